```python
import math, functools
import jax, jax.numpy as jnp
from jax import lax
import numpy as np

D_MODEL = 4096
BATCH = 1
SEQ = 16384
DEPTH = 2

CTX_LEN = 256
GRID_W = 64
HEAD_DIM = 128
H_DN = (3 * D_MODEL) // (8 * HEAD_DIM)
H_GLA = (5 * D_MODEL) // (16 * HEAD_DIM)
H_RET = D_MODEL // HEAD_DIM - H_DN - H_GLA
GLA_DK = HEAD_DIM // 2
GLA_DV = HEAD_DIM
GLA_GATE_RANK = 16
GLA_GATE_NORM = 16.0
CONV_W = 5
CHUNK = 64
D_FF = 4 * D_MODEL
EPS = 1e-6
ROPE_BASE = 10000.0

D_DN = H_DN * HEAD_DIM
D_GLA_K = H_GLA * GLA_DK
D_GLA_V = H_GLA * GLA_DV
D_RET = H_RET * HEAD_DIM
D_MIX = D_DN + D_GLA_V + D_RET
SPLIT_SIZES = (D_DN, D_DN, D_DN, D_DN, H_DN, H_DN, H_DN, H_DN,
               D_GLA_K, D_GLA_K, D_GLA_V, D_GLA_V, GLA_GATE_RANK, GLA_GATE_RANK,
               D_RET, D_RET, D_RET, D_RET)
D_IN = sum(SPLIT_SIZES)

kernel_name = "hybrid_parallel_deltanet_gla_retention_dit"


def rmsnorm(x, g):
    xf = x.astype(jnp.float32)
    y = xf * lax.rsqrt(jnp.mean(xf * xf, axis=-1, keepdims=True) + EPS) * g.astype(jnp.float32)
    return y.astype(x.dtype)


def modulate(h, shift, scale):
    return h * (1 + scale) + shift


def heads(t, n):
    b, l, _ = t.shape
    return t.reshape(b, l, n, -1).transpose(0, 2, 1, 3)


def merge_heads(t):
    b, n, l, d = t.shape
    return t.transpose(0, 2, 1, 3).reshape(b, l, n * d)


def l2norm(t):
    t = t.astype(jnp.float32)
    return t * lax.rsqrt(jnp.sum(t * t, axis=-1, keepdims=True) + EPS)


def head_rmsnorm(o, g):
    o = o.astype(jnp.float32)
    return o * lax.rsqrt(jnp.mean(o * o, axis=-1, keepdims=True) + EPS) * g.astype(jnp.float32)


def head_groupnorm(o, g):
    o = o.astype(jnp.float32)
    mu = jnp.mean(o, axis=-1, keepdims=True)
    var = jnp.mean(jnp.square(o - mu), axis=-1, keepdims=True)
    return (o - mu) * lax.rsqrt(var + EPS) * g.astype(jnp.float32)


def short_conv(u, w):
    y = lax.conv_general_dilated(u, w[:, None, :].astype(u.dtype), (1,), [(CONV_W // 2, CONV_W // 2)],
                                 dimension_numbers=("NWC", "WIO", "NWC"),
                                 feature_group_count=u.shape[-1])
    return jax.nn.silu(y)


def rope_axis(t, pos):
    quarter = t.shape[-1] // 2
    inv = ROPE_BASE ** (-jnp.arange(quarter, dtype=jnp.float32) / quarter)
    ang = pos.astype(jnp.float32)[:, None] * inv
    cos, sin = jnp.cos(ang), jnp.sin(ang)
    t1, t2 = t[..., :quarter], t[..., quarter:]
    return jnp.concatenate([t1 * cos - t2 * sin, t1 * sin + t2 * cos], axis=-1)


def rope_2d(t, row, col):
    t = t.astype(jnp.float32)
    half = t.shape[-1] // 2
    return jnp.concatenate([rope_axis(t[..., :half], row), rope_axis(t[..., half:], col)], axis=-1)


def to_chunks(t):
    b, h, l = t.shape[:3]
    return t.astype(jnp.float32).reshape(b, h, l // CHUNK, CHUNK, *t.shape[3:])


def chunk_state_scan(k_dec, v, g_last, s0):
    upd = jnp.einsum("bhncd,bhnce->bhnde", k_dec, v)

    def step(s, xc):
        gl, u = xc
        return gl[..., :, None] * s + u, s

    s_fin, s_prev = lax.scan(step, s0, (jnp.moveaxis(g_last, 2, 0), jnp.moveaxis(upd, 2, 0)))
    return jnp.moveaxis(s_prev, 0, 2), s_fin


def gated_delta_chunked(q, k, v, beta, g, s0):
    b, h, l, dk = q.shape
    dv = v.shape[-1]
    q, k, v = to_chunks(q), to_chunks(k), to_chunks(v)
    beta, g = to_chunks(beta), to_chunks(g)
    idx = jnp.arange(CHUNK)
    tri = idx[:, None] >= idx[None, :]
    strict = idx[:, None] > idx[None, :]
    eye = jnp.eye(CHUNK, dtype=jnp.float32)
    gc = jnp.cumsum(g, axis=-1)
    decay = jnp.exp(jnp.where(tri, gc[..., :, None] - gc[..., None, :], -jnp.inf))
    k_beta = k * beta[..., None]
    a = jnp.where(strict, jnp.einsum("bhnid,bhnjd->bhnij", k_beta, k) * decay, 0.0)
    t_inv = lax.linalg.triangular_solve(eye + a, jnp.broadcast_to(eye, a.shape),
                                        left_side=True, lower=True)
    w = jnp.einsum("bhnij,bhnjd->bhnid", t_inv, k_beta * jnp.exp(gc)[..., None])
    u = jnp.einsum("bhnij,bhnje->bhnie", t_inv, v * beta[..., None])
    qk = jnp.einsum("bhnid,bhnjd->bhnij", q, k) * decay
    q_dec = q * jnp.exp(gc)[..., None]
    k_dec = k * jnp.exp(gc[..., -1:] - gc)[..., None]
    g_last = jnp.exp(gc[..., -1])
    xs = tuple(jnp.moveaxis(a_, 2, 0) for a_ in (w, u, qk, q_dec, k_dec, g_last))

    def step(s, xc):
        w_c, u_c, qk_c, qd_c, kd_c, gl_c = xc
        v_new = u_c - jnp.einsum("bhcd,bhde->bhce", w_c, s)
        o_c = jnp.einsum("bhcd,bhde->bhce", qd_c, s) + jnp.einsum("bhij,bhje->bhie", qk_c, v_new)
        s = gl_c[..., None, None] * s + jnp.einsum("bhcd,bhce->bhde", kd_c, v_new)
        return s, o_c

    s_fin, o = lax.scan(step, s0, xs)
    return jnp.moveaxis(o, 0, 2).reshape(b, h, l, dv), s_fin


def gla_chunked(q, k, v, log_a, s0):
    b, h, l, dk = q.shape
    dv = v.shape[-1]
    q, k, v, log_a = to_chunks(q), to_chunks(k), to_chunks(v), to_chunks(log_a)
    idx = jnp.arange(CHUNK)
    tri = idx[:, None] >= idx[None, :]
    gc = jnp.cumsum(log_a, axis=-2)
    g_last = gc[..., -1, :]
    q_in = q * jnp.exp(gc)
    k_in = k * jnp.exp(-gc)
    scores = jnp.where(tri, jnp.einsum("bhnid,bhnjd->bhnij", q_in, k_in), 0.0)
    intra = jnp.einsum("bhnij,bhnje->bhnie", scores, v)
    k_dec = k * jnp.exp(g_last[..., None, :] - gc)
    s_prev, s_fin = chunk_state_scan(k_dec, v, jnp.exp(g_last), s0)
    o = intra + jnp.einsum("bhnid,bhnde->bhnie", q_in, s_prev)
    return o.reshape(b, h, l, dv), s_fin


def retention_chunked(q, k, v, s0, log_gamma):
    b, h, l, dk = q.shape
    dv = v.shape[-1]
    q, k, v = to_chunks(q), to_chunks(k), to_chunks(v)
    pos = jnp.arange(CHUNK, dtype=jnp.float32)
    tri = pos[:, None] >= pos[None, :]
    lg = log_gamma.astype(jnp.float32)[:, None]
    decay = jnp.exp(jnp.where(tri, lg[:, :, None] * (pos[:, None] - pos[None, :]), -jnp.inf))
    scores = jnp.einsum("bhnid,bhnjd->bhnij", q, k) * decay[None, :, None]
    intra = jnp.einsum("bhnij,bhnje->bhnie", scores, v)
    q_dec = q * jnp.exp(lg * (pos + 1))[None, :, None, :, None]
    k_dec = k * jnp.exp(lg * (CHUNK - 1 - pos))[None, :, None, :, None]
    g_last = jnp.broadcast_to(jnp.exp(lg * CHUNK)[None, :, None, :], (b, h, q.shape[2], dk))
    s_prev, s_fin = chunk_state_scan(k_dec, v, g_last, s0)
    o = intra + jnp.einsum("bhnid,bhnde->bhnie", q_dec, s_prev)
    return o.reshape(b, h, l, dv), s_fin


def flip_t(a):
    return jnp.flip(a, axis=2)


def run_bidir(scan_fn, ctx_tok, lat_tok, s0):
    (c_sh, c_f, c_b), (l_sh, l_f, l_b) = ctx_tok, lat_tok
    o_cf, s_cf = scan_fn(*c_sh, *c_f, s0)
    o_cb, s_cb = scan_fn(*[flip_t(a) for a in c_sh], *[flip_t(a) for a in c_b], s0)
    o_lf, _ = scan_fn(*l_sh, *l_f, s_cf)
    o_lb, _ = scan_fn(*[flip_t(a) for a in l_sh], *[flip_t(a) for a in l_b], s_cb)
    return o_cf + flip_t(o_cb), o_lf + flip_t(o_lb)


def prepare_stream(h, w_in, conv_w, a_log, dt_bias, gate_w, gate_b, rope_pos):
    f32 = jnp.float32
    points = np.cumsum(SPLIT_SIZES)[:-1].tolist()
    (dq, dk, dv, dz, da_f, da_b, db_f, db_b,
     gq, gk, gv, gg, glr_f, glr_b,
     rq, rk, rv, rg) = jnp.split(h @ w_in, points, axis=-1)
    qkv = short_conv(jnp.concatenate([dq, dk, dv], axis=-1), conv_w)
    cq, ck, cv = jnp.split(qkv, 3, axis=-1)
    dn_q = l2norm(heads(cq, H_DN)) * HEAD_DIM ** -0.5
    dn_k = l2norm(heads(ck, H_DN))
    dn_v = heads(cv, H_DN).astype(f32)

    def dn_gates(da, db, d):
        beta = jax.nn.sigmoid(db.astype(f32)).transpose(0, 2, 1)
        g = -jnp.exp(a_log[d].astype(f32)) * jax.nn.softplus(da.astype(f32) + dt_bias[d].astype(f32))
        return beta, g.transpose(0, 2, 1)

    dn_tok = ((dn_q, dn_k, dn_v), dn_gates(da_f, db_f, 0), dn_gates(da_b, db_b, 1))
    gla_q = heads(gq, H_GLA).astype(f32) * GLA_DK ** -0.5
    gla_k = heads(gk, H_GLA).astype(f32)
    gla_v = heads(gv, H_GLA).astype(f32)

    def gla_gate(glr, d):
        z = glr.astype(f32) @ gate_w[d].astype(f32) + gate_b[d].astype(f32)
        return (heads(jax.nn.log_sigmoid(z) / GLA_GATE_NORM, H_GLA),)

    gla_tok = ((gla_q, gla_k, gla_v), gla_gate(glr_f, 0), gla_gate(glr_b, 1))
    ret_q = heads(rq, H_RET).astype(f32) * HEAD_DIM ** -0.5
    ret_k = heads(rk, H_RET).astype(f32)
    if rope_pos is not None:
        ret_q = rope_2d(ret_q, *rope_pos)
        ret_k = rope_2d(ret_k, *rope_pos)
    ret_tok = ((ret_q, ret_k, heads(rv, H_RET).astype(f32)), (), ())
    return dn_tok, gla_tok, ret_tok, (dz, gg, rg)


def finish_stream(o_dn, o_gla, o_ret, gates, dn_g, gla_g, ret_g, w_out):
    z_dn, g_gla, g_ret = gates
    f32 = jnp.float32
    y = jnp.concatenate([
        merge_heads(head_rmsnorm(o_dn, dn_g)) * jax.nn.silu(z_dn.astype(f32)),
        merge_heads(head_rmsnorm(o_gla, gla_g)) * jax.nn.silu(g_gla.astype(f32)),
        merge_heads(head_groupnorm(o_ret, ret_g)) * jax.nn.silu(g_ret.astype(f32)),
    ], axis=-1)
    return y.astype(w_out.dtype) @ w_out


def sq_relu_mlp(h, w1, w2):
    return jnp.square(jax.nn.relu(h @ w1)) @ w2


def setup_inputs(seed: int = 0) -> dict:
    key = jax.random.key(seed)
    ks = jax.random.split(key, 24)
    f32 = jnp.float32

    def nrm(k, shape, scale):
        return jax.random.normal(k, shape, f32) * scale

    def gain(k, shape):
        return 1.0 + 0.02 * jax.random.normal(k, shape, f32)

    dt = jnp.exp(jax.random.uniform(ks[9], (DEPTH, 2, H_DN), f32, math.log(1e-3), math.log(1e-1)))
    return {
        "x": nrm(ks[0], (BATCH, SEQ, D_MODEL), 1.0),
        "c": nrm(ks[1], (BATCH, D_MODEL), 1.0),
        "ctx": nrm(ks[2], (BATCH, CTX_LEN, D_MODEL), 1.0),
        "c_ctx": nrm(ks[3], (D_MODEL,), 1.0),
        "w_ada": nrm(ks[4], (DEPTH, D_MODEL, 6 * D_MODEL), D_MODEL ** -0.5),
        "b_ada": nrm(ks[5], (DEPTH, 6 * D_MODEL), 0.02),
        "norm1_g": gain(ks[6], (DEPTH, D_MODEL)),
        "w_in": nrm(ks[7], (DEPTH, D_MODEL, D_IN), D_MODEL ** -0.5),
        "dn_conv": nrm(ks[8], (DEPTH, CONV_W, 3 * D_DN), CONV_W ** -0.5),
        "dn_a_log": jnp.log(jax.random.uniform(ks[10], (DEPTH, 2, H_DN), f32, 1.0, 16.0)),
        "dn_dt_bias": dt + jnp.log(-jnp.expm1(-dt)),
        "dn_norm_g": gain(ks[11], (DEPTH, HEAD_DIM)),
        "gla_gate_w": nrm(ks[12], (DEPTH, 2, GLA_GATE_RANK, D_GLA_K), GLA_GATE_RANK ** -0.5),
        "gla_gate_b": nrm(ks[13], (DEPTH, 2, D_GLA_K), 0.1),
        "gla_norm_g": gain(ks[14], (DEPTH, GLA_DV)),
        "ret_norm_g": gain(ks[15], (DEPTH, HEAD_DIM)),
        "w_out": nrm(ks[16], (DEPTH, D_MIX, D_MODEL), D_MIX ** -0.5),
        "norm2_g": gain(ks[17], (DEPTH, D_MODEL)),
        "w_ff1": nrm(ks[18], (DEPTH, D_MODEL, D_FF), D_MODEL ** -0.5),
        "w_ff2": nrm(ks[19], (DEPTH, D_FF, D_MODEL), D_FF ** -0.5),
        "final_norm_g": gain(ks[20], (D_MODEL,)),
    }


def reference(x, c, ctx, c_ctx, w_ada, b_ada, norm1_g, w_in, dn_conv, dn_a_log, dn_dt_bias, dn_norm_g,
              gla_gate_w, gla_gate_b, gla_norm_g, ret_norm_g, w_out, norm2_g, w_ff1, w_ff2, final_norm_g):
    b, l = x.shape[0], x.shape[1]
    ROWS = l // GRID_W
    row = jnp.repeat(jnp.arange(ROWS, dtype=jnp.int32), GRID_W)
    col = jnp.tile(jnp.arange(GRID_W, dtype=jnp.int32), ROWS)
    log_gamma = jnp.log1p(-jnp.exp2(-5.0 - jnp.arange(H_RET, dtype=jnp.float32)))
    ret_fn = functools.partial(retention_chunked, log_gamma=log_gamma)
    s0_dn = jnp.zeros((b, H_DN, HEAD_DIM, HEAD_DIM), jnp.float32)
    s0_gla = jnp.zeros((b, H_GLA, GLA_DK, GLA_DV), jnp.float32)
    s0_ret = jnp.zeros((b, H_RET, HEAD_DIM, HEAD_DIM), jnp.float32)
    y_ctx = ctx
    for i in range(DEPTH):
        mod_l = (jax.nn.silu(c) @ w_ada[i] + b_ada[i])[:, None, :]
        mod_c = (jax.nn.silu(c_ctx) @ w_ada[i] + b_ada[i])[None, None, :]
        sh1_l, sc1_l, g1_l, sh2_l, sc2_l, g2_l = jnp.split(mod_l, 6, axis=-1)
        sh1_c, sc1_c, g1_c, sh2_c, sc2_c, g2_c = jnp.split(mod_c, 6, axis=-1)
        h_l = modulate(rmsnorm(x, norm1_g[i]), sh1_l, sc1_l)
        h_c = modulate(rmsnorm(y_ctx, norm1_g[i]), sh1_c, sc1_c)
        p = (w_in[i], dn_conv[i], dn_a_log[i], dn_dt_bias[i], gla_gate_w[i], gla_gate_b[i])
        dn_c, gla_c, ret_c, gates_c = prepare_stream(h_c, *p, None)
        dn_l, gla_l, ret_l, gates_l = prepare_stream(h_l, *p, (row, col))
        o_dn_c, o_dn_l = run_bidir(gated_delta_chunked, dn_c, dn_l, s0_dn)
        o_gla_c, o_gla_l = run_bidir(gla_chunked, gla_c, gla_l, s0_gla)
        o_ret_c, o_ret_l = run_bidir(ret_fn, ret_c, ret_l, s0_ret)
        q = (dn_norm_g[i], gla_norm_g[i], ret_norm_g[i], w_out[i])
        x = x + g1_l * finish_stream(o_dn_l, o_gla_l, o_ret_l, gates_l, *q)
        x = x + g2_l * sq_relu_mlp(modulate(rmsnorm(x, norm2_g[i]), sh2_l, sc2_l), w_ff1[i], w_ff2[i])
        if i < DEPTH - 1:
            y_ctx = y_ctx + g1_c * finish_stream(o_dn_c, o_gla_c, o_ret_c, gates_c, *q)
            y_ctx = y_ctx + g2_c * sq_relu_mlp(modulate(rmsnorm(y_ctx, norm2_g[i]), sh2_c, sc2_c),
                                               w_ff1[i], w_ff2[i])
    return rmsnorm(x, final_norm_g)
```

```python
import functools
import math

import numpy as np
import jax
import jax.numpy as jnp
from jax import lax
from jax.experimental import pallas as pl
from jax.experimental.pallas import tpu as pltpu

F32 = jnp.float32
BF16 = jnp.bfloat16

HEAD_DIM = 128
CHUNK = 64
GRID_W = 64
CONV_W = 5
CONV_HALO = 16
GLA_GATE_RANK = 16
GLA_GATE_NORM = 16.0
EPS = 1e-6
ROPE_BASE = 10000.0
LANES = 128
BLK = 4 * CHUNK
VMEM_LIMIT = 60000 * 1024
MM_VMEM_BUDGET = 52 * 1024 * 1024


def _cparams(sem):
    return pltpu.CompilerParams(dimension_semantics=sem, vmem_limit_bytes=VMEM_LIMIT)


def _dot(a, b):
    return jnp.dot(a, b, preferred_element_type=F32)


def _dot_nt(a, b):
    return lax.dot_general(a, b, (((1,), (1,)), ((), ())), preferred_element_type=F32)


def _dot_tn(a, b):
    return lax.dot_general(a, b, (((0,), (0,)), ((), ())), preferred_element_type=F32)


def _split3(a):
    hi = a.astype(BF16)
    r = a - hi.astype(F32)
    mid = r.astype(BF16)
    lo = (r - mid.astype(F32)).astype(BF16)
    return hi, mid, lo


def _dot_exact_rhs(a, m_bf16):
    hi, mid, lo = _split3(a)
    return _dot(hi, m_bf16) + (_dot(mid, m_bf16) + _dot(lo, m_bf16))


def _dot_exact_lhs(m_bf16, a):
    hi, mid, lo = _split3(a)
    return _dot(m_bf16, hi) + (_dot(m_bf16, mid) + _dot(m_bf16, lo))


def _dot_f32(a, b):
    a_hi = a.astype(BF16)
    a_lo = (a - a_hi.astype(F32)).astype(BF16)
    b_hi = b.astype(BF16)
    b_lo = (b - b_hi.astype(F32)).astype(BF16)
    return _dot(a_hi, b_hi) + (_dot(a_hi, b_lo) + _dot(a_lo, b_hi))


def _softplus(x):
    return jnp.maximum(x, 0.0) + jnp.log1p(jnp.exp(-jnp.abs(x)))


def _sigmoid(x):
    return 1.0 / (1.0 + jnp.exp(-x))


def _silu(x):
    return x * _sigmoid(x)


def _tri_masks(reverse):
    ii = lax.broadcasted_iota(jnp.int32, (CHUNK, CHUNK), 0)
    jj = lax.broadcasted_iota(jnp.int32, (CHUNK, CHUNK), 1)
    if reverse:
        return ii <= jj, ii < jj, ii == jj
    return ii >= jj, ii > jj, ii == jj


def _to_col(row, eye):
    return jnp.sum(jnp.where(eye, row, 0.0), axis=1, keepdims=True)


def _ada_kernel(s_ref, w_ref, b_ref, o_ref):
    s = _silu(s_ref[...]).astype(BF16)
    o_ref[...] = _dot(s, w_ref[...].astype(BF16)) + b_ref[...]


def _ada(c2, w_ada, b_ada):
    depth, d, n = w_ada.shape
    tn = 1024
    return pl.pallas_call(
        _ada_kernel,
        grid=(depth, n // tn),
        in_specs=[
            pl.BlockSpec((8, d), lambda l, j: (0, 0)),
            pl.BlockSpec((None, d, tn), lambda l, j: (l, 0, j)),
            pl.BlockSpec((None, 1, tn), lambda l, j: (l, 0, j)),
        ],
        out_specs=pl.BlockSpec((None, 8, tn), lambda l, j: (l, 0, j)),
        out_shape=jax.ShapeDtypeStruct((depth, 8, n), F32),
        compiler_params=_cparams(("arbitrary", "arbitrary")),
        name="ada_mod",
    )(c2, w_ada, b_ada.reshape(depth, 1, n))


def _norm_mod_kernel(x_ref, g_ref, sh_ref, sc_ref, o_ref, *, n_ctx_blk):
    is_ctx = pl.program_id(0) < n_ctx_blk
    x = x_ref[...]
    y = x * lax.rsqrt(jnp.mean(x * x, axis=-1, keepdims=True) + EPS) * g_ref[...]
    sh = jnp.where(is_ctx, sh_ref[1:2, :], sh_ref[0:1, :])
    sc = jnp.where(is_ctx, sc_ref[1:2, :], sc_ref[0:1, :])
    o_ref[...] = (y * (1.0 + sc) + sh).astype(o_ref.dtype)


def _norm_mod(x_all, gain, mod, layer, which_shift, which_scale, n_ctx_blk):
    lt, d = x_all.shape
    return pl.pallas_call(
        functools.partial(_norm_mod_kernel, n_ctx_blk=n_ctx_blk),
        grid=(lt // BLK,),
        in_specs=[
            pl.BlockSpec((BLK, d), lambda i: (i, 0)),
            pl.BlockSpec((1, d), lambda i: (0, 0)),
            pl.BlockSpec((None, 8, d), lambda i: (layer, 0, which_shift)),
            pl.BlockSpec((None, 8, d), lambda i: (layer, 0, which_scale)),
        ],
        out_specs=pl.BlockSpec((BLK, d), lambda i: (i, 0)),
        out_shape=jax.ShapeDtypeStruct((lt, d), BF16),
        compiler_params=_cparams(("arbitrary",)),
        name="norm_mod",
    )(x_all, gain.reshape(1, d), mod, mod)


def _final_norm_kernel(x_ref, g_ref, o_ref):
    x = x_ref[...]
    o_ref[...] = x * lax.rsqrt(jnp.mean(x * x, axis=-1, keepdims=True) + EPS) * g_ref[...]


def _final_norm(x_all, gain, n_ctx_blk):
    lt, d = x_all.shape
    n_lat = lt - n_ctx_blk * BLK
    return pl.pallas_call(
        _final_norm_kernel,
        grid=(n_lat // BLK,),
        in_specs=[
            pl.BlockSpec((BLK, d), lambda i: (i + n_ctx_blk, 0)),
            pl.BlockSpec((1, d), lambda i: (0, 0)),
        ],
        out_specs=pl.BlockSpec((BLK, d), lambda i: (i, 0)),
        out_shape=jax.ShapeDtypeStruct((n_lat, d), F32),
        compiler_params=_cparams(("arbitrary",)),
        name="final_norm",
    )(x_all, gain.reshape(1, d))


def _mm_epilogue(acc, mode, x_ref, gate_ref, o_ref, row0, n_ctx_rows):
    if mode == "plain":
        o_ref[...] = acc.astype(o_ref.dtype)
    elif mode == "relu2":
        r = jnp.maximum(acc, 0.0)
        o_ref[...] = (r * r).astype(o_ref.dtype)
    else:
        row = row0 + lax.broadcasted_iota(jnp.int32, (acc.shape[0], 1), 0)
        gate = jnp.where(row < n_ctx_rows, gate_ref[1:2, :], gate_ref[0:1, :])
        o_ref[...] = x_ref[...] + gate * acc


def _mm_kernel(*refs, mode, nk, tm, n_ctx_rows):
    if mode == "resid":
        a_ref, b_ref, x_ref, gate_ref = refs[:4]
        rest = refs[4:]
    else:
        a_ref, b_ref = refs[:2]
        x_ref = gate_ref = None
        rest = refs[2:]
    o_ref = rest[0]
    row0 = pl.program_id(1) * tm
    if nk == 1:
        acc = _dot(a_ref[...], b_ref[...])
        _mm_epilogue(acc, mode, x_ref, gate_ref, o_ref, row0, n_ctx_rows)
        return
    acc_ref = rest[1]
    k = pl.program_id(2)

    @pl.when(k == 0)
    def _():
        acc_ref[...] = jnp.zeros_like(acc_ref)

    acc_ref[...] += _dot(a_ref[...], b_ref[...])

    @pl.when(k == nk - 1)
    def _():
        _mm_epilogue(acc_ref[...], mode, x_ref, gate_ref, o_ref, row0, n_ctx_rows)


def _mm_tiles(m, n, k, mode):
    out_bytes = 4 if mode == "resid" else 2
    for tm in (1280, 1024, 768, 512, 256):
        if m % tm:
            continue
        for tn, tk in ((1024, 4096), (768, 4096), (1024, 2048), (512, 4096), (512, 2048), (512, 1024),
                       (256, 1024)):
            if n % tn or k % min(tk, k):
                continue
            tk = min(tk, k)
            use = 2 * (tm * tk * 2 + tk * tn * 2 + tm * tn * out_bytes) + tm * tn * 4
            if mode == "resid":
                use += 2 * tm * tn * 4
            if k > tk:
                use += tm * tn * 4
            if use <= MM_VMEM_BUDGET:
                return tm, tn, tk
    raise ValueError(f"no matmul tiling for {(m, n, k)}")


def _matmul(a, b, mode="plain", x=None, mod=None, layer=0, which_gate=0, n_ctx_rows=0):
    m, k = a.shape
    _, n = b.shape
    tm, tn, tk = _mm_tiles(m, n, k, mode)
    nk = k // tk
    in_specs = [
        pl.BlockSpec((tm, tk), lambda j, i, kk: (i, kk)),
        pl.BlockSpec((tk, tn), lambda j, i, kk: (kk, j)),
    ]
    args = [a, b]
    if mode == "resid":
        gate_blk0 = which_gate * (n // tn)
        in_specs += [
            pl.BlockSpec((tm, tn), lambda j, i, kk: (i, j)),
            pl.BlockSpec((None, 8, tn), lambda j, i, kk: (layer, 0, gate_blk0 + j)),
        ]
        args += [x, mod]
    scratch = [pltpu.VMEM((tm, tn), F32)] if nk > 1 else []
    return pl.pallas_call(
        functools.partial(_mm_kernel, mode=mode, nk=nk, tm=tm, n_ctx_rows=n_ctx_rows),
        grid=(n // tn, m // tm, nk),
        in_specs=in_specs,
        out_specs=pl.BlockSpec((tm, tn), lambda j, i, kk: (i, j)),
        out_shape=jax.ShapeDtypeStruct((m, n), F32 if mode == "resid" else BF16),
        scratch_shapes=scratch,
        compiler_params=_cparams(("arbitrary", "arbitrary", "arbitrary")),
        name="mm_" + mode,
    )(*args)


def _gates_t_kernel(w_ref, h_ref, o_ref):
    o_ref[...] = _dot_nt(w_ref[...], h_ref[...])


def _gates_t(ws_t, h):
    lt, d = h.shape
    tm = 1280 if lt % 1280 == 0 else BLK
    return pl.pallas_call(
        _gates_t_kernel,
        grid=(lt // tm,),
        in_specs=[
            pl.BlockSpec((LANES, d), lambda i: (0, 0)),
            pl.BlockSpec((tm, d), lambda i: (i, 0)),
        ],
        out_specs=pl.BlockSpec((LANES, tm), lambda i: (0, i)),
        out_shape=jax.ShapeDtypeStruct((LANES, lt), F32),
        compiler_params=_cparams(("arbitrary",)),
        name="gates_t",
    )(ws_t, h)


def _dn_prep_kernel(prev_ref, main_ref, next_ref, w_ref, o_ref, ext_ref, *, n_ctx_blk, n_blk, n_heads):
    i = pl.program_id(0)
    prev_ok = jnp.logical_and(i != 0, i != n_ctx_blk)
    next_ok = jnp.logical_and(i != n_ctx_blk - 1, i != n_blk - 1)
    ext_ref[0:CONV_HALO, :] = jnp.where(prev_ok, prev_ref[...].astype(F32), 0.0)
    ext_ref[CONV_HALO:CONV_HALO + BLK, :] = main_ref[...].astype(F32)
    ext_ref[CONV_HALO + BLK:, :] = jnp.where(next_ok, next_ref[...].astype(F32), 0.0)
    for g in range(3 * n_heads):
        cs = slice(g * HEAD_DIM, (g + 1) * HEAD_DIM)
        acc = None
        for tap in range(CONV_W):
            win = ext_ref[pl.ds(CONV_HALO - CONV_W // 2 + tap, BLK), cs]
            term = win * w_ref[tap:tap + 1, cs]
            acc = term if acc is None else acc + term
        y = _silu(acc)
        if g < 2 * n_heads:
            y = y * lax.rsqrt(jnp.sum(y * y, axis=-1, keepdims=True) + EPS)
            if g < n_heads:
                y = y * HEAD_DIM ** -0.5
        o_ref[:, cs] = y.astype(o_ref.dtype)


def _dn_prep(proj, conv_w, n_heads, n_ctx_blk):
    lt = proj.shape[0]
    c = 3 * n_heads * HEAD_DIM
    n_blk = lt // BLK
    per = BLK // CONV_HALO
    n_halo = lt // CONV_HALO
    return pl.pallas_call(
        functools.partial(_dn_prep_kernel, n_ctx_blk=n_ctx_blk, n_blk=n_blk, n_heads=n_heads),
        grid=(n_blk,),
        in_specs=[
            pl.BlockSpec((CONV_HALO, c), lambda i: (jnp.maximum(i * per - 1, 0), 0)),
            pl.BlockSpec((BLK, c), lambda i: (i, 0)),
            pl.BlockSpec((CONV_HALO, c), lambda i: (jnp.minimum((i + 1) * per, n_halo - 1), 0)),
            pl.BlockSpec((CONV_W, c), lambda i: (0, 0)),
        ],
        out_specs=pl.BlockSpec((BLK, c), lambda i: (i, 0)),
        out_shape=jax.ShapeDtypeStruct((lt, c), BF16),
        scratch_shapes=[pltpu.VMEM((BLK + 2 * CONV_HALO, c), F32)],
        compiler_params=_cparams(("arbitrary",)),
        name="dn_prep",
    )(proj, proj, proj, conv_w)


def _fwd_blk(s):
    return s


def _bwd_blk(s, n_ctx_blk, n_blk):
    return jnp.where(s < n_ctx_blk, n_ctx_blk - 1 - s, n_blk - 1 - (s - n_ctx_blk))


INV_BASE = 8


def _unit_triangular_inverse(a):
    ii = lax.broadcasted_iota(jnp.int32, (CHUNK, CHUNK), 0)
    jj = lax.broadcasted_iota(jnp.int32, (CHUNK, CHUNK), 1)
    d = jnp.where(ii // INV_BASE == jj // INV_BASE, a, 0.0)
    m = jnp.where(ii == jj, 1.0, 0.0) - d
    p = _dot_f32(d, d)
    size = 2
    while size < INV_BASE:
        m = m + _dot_f32(m, p)
        size *= 2
        if size < INV_BASE:
            p = _dot_f32(p, p)
    size = INV_BASE
    while size < CHUNK:
        off = jnp.where(jnp.logical_and(ii // (2 * size) == jj // (2 * size), ii // size != jj // size), a, 0.0)
        m = m - _dot_f32(_dot_f32(m, off), m)
        size *= 2
    return m


def _dn_direction(q_ref, k_ref, v_ref, da_ref, db_ref, neg_a, dt_bias, s_ref, o_ref, reverse):
    incl, strict, eye = _tri_masks(reverse)
    ii = lax.broadcasted_iota(jnp.int32, (CHUNK, CHUNK), 0)
    jj = lax.broadcasted_iota(jnp.int32, (CHUNK, CHUNK), 1)
    cum = jnp.where((ii >= jj) if reverse else (ii <= jj), 1.0, 0.0).astype(BF16)
    g = neg_a * _softplus(da_ref[...] + dt_bias)
    beta = _sigmoid(db_ref[...])
    gc_rows = _dot_exact_rhs(g, cum)
    n_chunks = BLK // CHUNK
    order = range(n_chunks - 1, -1, -1) if reverse else range(n_chunks)
    last = 0 if reverse else CHUNK - 1
    for c in order:
        rows = slice(c * CHUNK, (c + 1) * CHUNK)
        q = q_ref[rows, :]
        k = k_ref[rows, :]
        v = v_ref[rows, :]
        gc_r = gc_rows[c:c + 1, :]
        gc_c = _to_col(gc_r, eye)
        b_c = _to_col(beta[c:c + 1, :], eye)
        g_last = gc_r[:, last:last + 1]
        decay = jnp.exp(jnp.where(incl, gc_c - gc_r, -jnp.inf))
        a = jnp.where(strict, _dot_nt(k, k) * decay, 0.0) * b_c
        t_inv = _unit_triangular_inverse(a).astype(BF16)
        e_gc = jnp.exp(gc_c)
        kf = k.astype(F32)
        w = _dot(t_inv, (kf * (b_c * e_gc)).astype(BF16))
        u = _dot(t_inv, (v.astype(F32) * b_c).astype(BF16))
        qk = (_dot_nt(q, k) * decay).astype(BF16)
        q_dec = (q.astype(F32) * e_gc).astype(BF16)
        k_dec = (kf * jnp.exp(g_last - gc_c)).astype(BF16)
        s = s_ref[...]
        s_b = s.astype(BF16)
        v_new = (u - _dot(w.astype(BF16), s_b)).astype(BF16)
        o = _dot(q_dec, s_b) + _dot(qk, v_new)
        s_ref[...] = jnp.exp(g_last) * s + _dot_tn(k_dec, v_new)
        o_ref[rows, :] = o.astype(o_ref.dtype)


def _dn_scan_kernel(alog_ref, dtb_ref, qf, kf, vf, daf, dbf, qb, kb, vb, dab, dbb, of_ref, ob_ref, s_ref):
    h = pl.program_id(0)

    @pl.when(pl.program_id(1) == 0)
    def _():
        s_ref[...] = jnp.zeros_like(s_ref)

    neg_a_f = -jnp.exp(jnp.full((1, CHUNK), alog_ref[0, h], F32))
    neg_a_b = -jnp.exp(jnp.full((1, CHUNK), alog_ref[1, h], F32))
    _dn_direction(qf, kf, vf, daf, dbf, neg_a_f, dtb_ref[0, h], s_ref.at[0], of_ref, False)
    _dn_direction(qb, kb, vb, dab, dbb, neg_a_b, dtb_ref[1, h], s_ref.at[1], ob_ref, True)


def _dn_scan(dnp, gates4, a_log, dt_bias, n_heads, n_ctx_blk):
    lt = dnp.shape[0]
    n_blk = lt // BLK
    bwd = functools.partial(_bwd_blk, n_ctx_blk=n_ctx_blk, n_blk=n_blk)
    n_chunks = BLK // CHUNK

    def col(off, blk_fn):
        return pl.BlockSpec((BLK, HEAD_DIM), lambda h, s: (blk_fn(s), off + h))

    def gate(off, blk_fn):
        return pl.BlockSpec((None, None, n_chunks, CHUNK), lambda h, s: (off + h, blk_fn(s), 0, 0))

    smem = pl.BlockSpec(memory_space=pltpu.SMEM)
    out_sds = jax.ShapeDtypeStruct((lt, n_heads * HEAD_DIM), BF16)
    return pl.pallas_call(
        _dn_scan_kernel,
        grid=(n_heads, n_blk),
        in_specs=[
            smem, smem,
            col(0, _fwd_blk), col(n_heads, _fwd_blk), col(2 * n_heads, _fwd_blk),
            gate(0, _fwd_blk), gate(2 * n_heads, _fwd_blk),
            col(0, bwd), col(n_heads, bwd), col(2 * n_heads, bwd),
            gate(n_heads, bwd), gate(3 * n_heads, bwd),
        ],
        out_specs=[
            pl.BlockSpec((BLK, HEAD_DIM), lambda h, s: (s, h)),
            pl.BlockSpec((BLK, HEAD_DIM), lambda h, s: (bwd(s), h)),
        ],
        out_shape=[out_sds, out_sds],
        scratch_shapes=[pltpu.VMEM((2, HEAD_DIM, HEAD_DIM), F32)],
        compiler_params=_cparams(("arbitrary", "arbitrary")),
        name="dn_scan",
    )(a_log, dt_bias, dnp, dnp, dnp, gates4, gates4, dnp, dnp, dnp, gates4, gates4)


def _gla_direction(q_ref, k_ref, v_ref, sm_ref, gw_ref, gb_ref, st_ref, o_ref, reverse, dk):
    incl, _, _ = _tri_masks(reverse)
    ii = lax.broadcasted_iota(jnp.int32, (CHUNK, CHUNK), 0)
    jj = lax.broadcasted_iota(jnp.int32, (CHUNK, CHUNK), 1)
    cum = jnp.where((ii <= jj) if reverse else (ii >= jj), 1.0, 0.0).astype(BF16)
    lane = lax.broadcasted_iota(jnp.int32, (1, LANES), 1)
    z = _dot(sm_ref[...], gw_ref[...]) + gb_ref[...]
    log_a = (jnp.minimum(z, 0.0) - jnp.log1p(jnp.exp(-jnp.abs(z)))) * (1.0 / GLA_GATE_NORM)
    n_chunks = BLK // CHUNK
    order = range(n_chunks - 1, -1, -1) if reverse else range(n_chunks)
    last = 0 if reverse else CHUNK - 1
    n_sub = LANES // dk
    for c in order:
        rows = slice(c * CHUNK, (c + 1) * CHUNK)
        gc = _dot_exact_lhs(cum, log_a[rows, :])
        g_last = gc[last:last + 1, :]
        qf = q_ref[rows, :].astype(F32)
        kf = k_ref[rows, :].astype(F32)
        q_in = qf * jnp.exp(gc) * dk ** -0.5
        k_in = (kf * jnp.exp(-gc)).astype(BF16)
        k_dec = (kf * jnp.exp(g_last - gc)).astype(BF16)
        st = st_ref[...]
        st_b = st.astype(BF16)
        upd = None
        for sub in range(n_sub):
            sel = (lane // dk) == sub
            qm = jnp.where(sel, q_in, 0.0).astype(BF16)
            v = v_ref[rows, sub * HEAD_DIM:(sub + 1) * HEAD_DIM]
            scores = jnp.where(incl, _dot_nt(qm, k_in), 0.0).astype(BF16)
            o = _dot(scores, v) + _dot_nt(qm, st_b)
            o_ref[rows, sub * HEAD_DIM:(sub + 1) * HEAD_DIM] = o.astype(o_ref.dtype)
            u = _dot_tn(v, k_dec)
            upd = u if upd is None else jnp.where(sel, u, upd)
        st_ref[...] = st * jnp.exp(g_last) + upd


def _gla_scan_kernel(qf, kf, vf, smf, gwf, gbf, qb, kb, vb, smb, gwb, gbb, of_ref, ob_ref, st_ref, *, dk):
    @pl.when(pl.program_id(1) == 0)
    def _():
        st_ref[...] = jnp.zeros_like(st_ref)

    _gla_direction(qf, kf, vf, smf, gwf, gbf, st_ref.at[0], of_ref, False, dk)
    _gla_direction(qb, kb, vb, smb, gwb, gbb, st_ref.at[1], ob_ref, True, dk)


def _gla_scan(proj, gw_full, gate_b, layer, lay, n_ctx_blk):
    lt = proj.shape[0]
    n_blk = lt // BLK
    n_heads, dk = lay["h_gla"], lay["gla_dk"]
    n_sub = LANES // dk
    n_pairs = n_heads // n_sub
    vw = n_sub * HEAD_DIM
    bwd = functools.partial(_bwd_blk, n_ctx_blk=n_ctx_blk, n_blk=n_blk)
    q0, k0, v0, sm0 = lay["gq"] // LANES, lay["gk"] // LANES, lay["gv"] // vw, lay["small"] // LANES

    def specs(d, blk_fn):
        return [
            pl.BlockSpec((BLK, LANES), lambda p, s: (blk_fn(s), q0 + p)),
            pl.BlockSpec((BLK, LANES), lambda p, s: (blk_fn(s), k0 + p)),
            pl.BlockSpec((BLK, vw), lambda p, s: (blk_fn(s), v0 + p)),
            pl.BlockSpec((BLK, LANES), lambda p, s: (blk_fn(s), sm0)),
            pl.BlockSpec((None, None, LANES, LANES), lambda p, s: (layer, d, 0, p)),
            pl.BlockSpec((None, None, 1, LANES), lambda p, s: (layer, d, 0, p)),
        ]

    out_sds = jax.ShapeDtypeStruct((lt, n_heads * HEAD_DIM), BF16)
    return pl.pallas_call(
        functools.partial(_gla_scan_kernel, dk=dk),
        grid=(n_pairs, n_blk),
        in_specs=specs(0, _fwd_blk) + specs(1, bwd),
        out_specs=[
            pl.BlockSpec((BLK, vw), lambda p, s: (s, p)),
            pl.BlockSpec((BLK, vw), lambda p, s: (bwd(s), p)),
        ],
        out_shape=[out_sds, out_sds],
        scratch_shapes=[pltpu.VMEM((2, HEAD_DIM, LANES), F32)],
        compiler_params=_cparams(("arbitrary", "arbitrary")),
        name="gla_scan",
    )(proj, proj, proj, proj, gw_full, gate_b, proj, proj, proj, proj, gw_full, gate_b)


def _rope(t, cos, sin_signed, first_half):
    partner = jnp.where(first_half, pltpu.roll(t, LANES - HEAD_DIM // 4, 1), pltpu.roll(t, HEAD_DIM // 4, 1))
    return t * cos + partner * sin_signed


def _ret_direction(q_ref, k_ref, v_ref, cos_ref, sin_ref, lg, s_ref, o_ref, reverse):
    incl, _, _ = _tri_masks(reverse)
    ii = lax.broadcasted_iota(jnp.int32, (CHUNK, CHUNK), 0)
    jj = lax.broadcasted_iota(jnp.int32, (CHUNK, CHUNK), 1)
    dist = (jj - ii) if reverse else (ii - jj)
    decay = jnp.exp(jnp.where(incl, lg * dist.astype(F32), -jnp.inf))
    pos = lax.broadcasted_iota(jnp.int32, (CHUNK, 1), 0).astype(F32)
    if reverse:
        q_scale = jnp.exp(lg * (CHUNK - pos))
        k_scale = jnp.exp(lg * pos)
    else:
        q_scale = jnp.exp(lg * (pos + 1.0))
        k_scale = jnp.exp(lg * (CHUNK - 1.0 - pos))
    g_chunk = jnp.exp(jnp.full((1, LANES), lg * CHUNK, F32))
    lane = lax.broadcasted_iota(jnp.int32, (1, LANES), 1)
    first_half = (lane % (HEAD_DIM // 2)) < (HEAD_DIM // 4)
    n_chunks = BLK // CHUNK
    order = range(n_chunks - 1, -1, -1) if reverse else range(n_chunks)
    for c in order:
        rows = slice(c * CHUNK, (c + 1) * CHUNK)
        cos = cos_ref[rows, :]
        sin = sin_ref[rows, :]
        q = _rope(q_ref[rows, :].astype(F32) * HEAD_DIM ** -0.5, cos, sin, first_half)
        k = _rope(k_ref[rows, :].astype(F32), cos, sin, first_half)
        v = v_ref[rows, :]
        scores = (_dot_nt(q.astype(BF16), k.astype(BF16)) * decay).astype(BF16)
        s = s_ref[...]
        o = _dot(scores, v) + _dot((q * q_scale).astype(BF16), s.astype(BF16))
        s_ref[...] = g_chunk * s + _dot_tn((k * k_scale).astype(BF16), v)
        o_ref[rows, :] = o.astype(o_ref.dtype)


def _ret_scan_kernel(lg_ref, qf, kf, vf, cf, sf, qb, kb, vb, cb, sb, of_ref, ob_ref, s_ref):
    lg = lg_ref[pl.program_id(0)]

    @pl.when(pl.program_id(1) == 0)
    def _():
        s_ref[...] = jnp.zeros_like(s_ref)

    _ret_direction(qf, kf, vf, cf, sf, lg, s_ref.at[0], of_ref, False)
    _ret_direction(qb, kb, vb, cb, sb, lg, s_ref.at[1], ob_ref, True)


def _ret_scan(proj, cos_t, sin_t, log_gamma, lay, n_ctx_blk):
    lt = proj.shape[0]
    n_blk = lt // BLK
    n_heads = lay["h_ret"]
    bwd = functools.partial(_bwd_blk, n_ctx_blk=n_ctx_blk, n_blk=n_blk)
    q0, k0, v0 = lay["rq"] // HEAD_DIM, lay["rk"] // HEAD_DIM, lay["rv"] // HEAD_DIM

    def specs(blk_fn):
        return [
            pl.BlockSpec((BLK, HEAD_DIM), lambda h, s: (blk_fn(s), q0 + h)),
            pl.BlockSpec((BLK, HEAD_DIM), lambda h, s: (blk_fn(s), k0 + h)),
            pl.BlockSpec((BLK, HEAD_DIM), lambda h, s: (blk_fn(s), v0 + h)),
            pl.BlockSpec((BLK, HEAD_DIM), lambda h, s: (blk_fn(s), 0)),
            pl.BlockSpec((BLK, HEAD_DIM), lambda h, s: (blk_fn(s), 0)),
        ]

    out_sds = jax.ShapeDtypeStruct((lt, n_heads * HEAD_DIM), BF16)
    return pl.pallas_call(
        _ret_scan_kernel,
        grid=(n_heads, n_blk),
        in_specs=[pl.BlockSpec(memory_space=pltpu.SMEM)] + specs(_fwd_blk) + specs(bwd),
        out_specs=[
            pl.BlockSpec((BLK, HEAD_DIM), lambda h, s: (s, h)),
            pl.BlockSpec((BLK, HEAD_DIM), lambda h, s: (bwd(s), h)),
        ],
        out_shape=[out_sds, out_sds],
        scratch_shapes=[pltpu.VMEM((2, HEAD_DIM, HEAD_DIM), F32)],
        compiler_params=_cparams(("arbitrary", "arbitrary")),
        name="ret_scan",
    )(log_gamma, proj, proj, proj, cos_t, sin_t, proj, proj, proj, cos_t, sin_t)


def _finish_kernel(dnf, dnb, dz, glf, glb, gg, rtf, rtb, rg, g_dn, g_gla, g_ret, o_ref, *, h_dn, h_gla, h_ret):
    col = 0
    for (of, ob, gate, gain, n_heads, centred) in (
        (dnf, dnb, dz, g_dn, h_dn, False),
        (glf, glb, gg, g_gla, h_gla, False),
        (rtf, rtb, rg, g_ret, h_ret, True),
    ):
        for h in range(n_heads):
            cs = slice(h * HEAD_DIM, (h + 1) * HEAD_DIM)
            o = of[:, cs].astype(F32) + ob[:, cs].astype(F32)
            if centred:
                o = o - jnp.mean(o, axis=-1, keepdims=True)
            y = o * lax.rsqrt(jnp.mean(o * o, axis=-1, keepdims=True) + EPS) * gain[...]
            y = y * _silu(gate[:, cs].astype(F32))
            o_ref[:, col:col + HEAD_DIM] = y.astype(o_ref.dtype)
            col += HEAD_DIM


def _finish(proj, dn_o, gla_o, ret_o, g_dn, g_gla, g_ret, lay):
    lt = proj.shape[0]
    h_dn, h_gla, h_ret = lay["h_dn"], lay["h_gla"], lay["h_ret"]
    w_dn, w_gla, w_ret = h_dn * HEAD_DIM, h_gla * HEAD_DIM, h_ret * HEAD_DIM
    d_mix = w_dn + w_gla + w_ret

    def rows(w, cblk=0):
        return pl.BlockSpec((BLK, w), lambda i: (i, cblk))

    gain = pl.BlockSpec((1, HEAD_DIM), lambda i: (0, 0))
    return pl.pallas_call(
        functools.partial(_finish_kernel, h_dn=h_dn, h_gla=h_gla, h_ret=h_ret),
        grid=(lt // BLK,),
        in_specs=[
            rows(w_dn), rows(w_dn), rows(w_dn, lay["dz"] // w_dn),
            rows(w_gla), rows(w_gla), rows(w_gla, lay["gg"] // w_gla),
            rows(w_ret), rows(w_ret), rows(w_ret, lay["rg"] // w_ret),
            gain, gain, gain,
        ],
        out_specs=rows(d_mix),
        out_shape=jax.ShapeDtypeStruct((lt, d_mix), BF16),
        compiler_params=_cparams(("arbitrary",)),
        name="finish",
    )(dn_o[0], dn_o[1], proj, gla_o[0], gla_o[1], proj, ret_o[0], ret_o[1], proj,
      g_dn.reshape(1, HEAD_DIM), g_gla.reshape(1, HEAD_DIM), g_ret.reshape(1, HEAD_DIM))


def _layout(d_model):
    n_heads = d_model // HEAD_DIM
    h_dn = (3 * d_model) // (8 * HEAD_DIM)
    h_gla = (5 * d_model) // (16 * HEAD_DIM)
    h_ret = n_heads - h_dn - h_gla
    gla_dk = HEAD_DIM // 2
    d_dn, d_gk, d_gv, d_ret = h_dn * HEAD_DIM, h_gla * gla_dk, h_gla * HEAD_DIM, h_ret * HEAD_DIM
    sizes = dict(dq=d_dn, dk=d_dn, dv=d_dn, dz=d_dn, da_f=h_dn, da_b=h_dn, db_f=h_dn, db_b=h_dn,
                 gq=d_gk, gk=d_gk, gv=d_gv, gg=d_gv, glr_f=GLA_GATE_RANK, glr_b=GLA_GATE_RANK,
                 rq=d_ret, rk=d_ret, rv=d_ret, rg=d_ret)
    src, off = {}, 0
    for name, size in sizes.items():
        src[name] = (off, size)
        off += size
    assert d_gv == d_ret == 2 * d_gk and 4 * h_dn + 2 * GLA_GATE_RANK <= LANES
    lay = dict(h_dn=h_dn, h_gla=h_gla, h_ret=h_ret, gla_dk=gla_dk, src=src, d_in=off)
    pos = 0
    order = []
    for name in ("dq", "dk", "dv", "dz"):
        lay[name] = pos
        order.append((name, pos))
        pos += d_dn
    lay["small"] = pos
    small_names = ("da_f", "da_b", "db_f", "db_b", "glr_f", "glr_b")
    sp = pos
    for name in small_names:
        lay[name] = sp
        order.append((name, sp))
        sp += sizes[name]
    pos += LANES
    pos = -(-pos // d_gv) * d_gv
    for name in ("gq", "gk"):
        lay[name] = pos
        order.append((name, pos))
        pos += d_gk
    for name in ("gv", "gg", "rq", "rk", "rv", "rg"):
        lay[name] = pos
        order.append((name, pos))
        pos += d_gv
    lay["n_cols"] = pos
    lay["order"] = order
    lay["glr_lane"] = (lay["glr_f"] - lay["small"], lay["glr_b"] - lay["small"])
    return lay


def _relayout_w_in(w_in, lay):
    depth, d, _ = w_in.shape
    out = jnp.zeros((depth, d, lay["n_cols"]), BF16)
    for name, pos in lay["order"]:
        s0, size = lay["src"][name]
        out = lax.dynamic_update_slice(out, w_in[:, :, s0:s0 + size].astype(BF16), (0, 0, pos))
    return out


def _rope_tables(n_ctx, n_lat):
    quarter = HEAD_DIM // 4
    inv = ROPE_BASE ** (-jnp.arange(quarter, dtype=F32) / quarter)
    t = jnp.arange(n_lat, dtype=jnp.int32)
    row = (t // GRID_W).astype(F32)[:, None] * inv
    colp = (t % GRID_W).astype(F32)[:, None] * inv
    cos = jnp.concatenate([jnp.cos(row), jnp.cos(row), jnp.cos(colp), jnp.cos(colp)], axis=-1)
    sin = jnp.concatenate([-jnp.sin(row), jnp.sin(row), -jnp.sin(colp), jnp.sin(colp)], axis=-1)
    cos = jnp.concatenate([jnp.ones((n_ctx, HEAD_DIM), F32), cos], axis=0)
    sin = jnp.concatenate([jnp.zeros((n_ctx, HEAD_DIM), F32), sin], axis=0)
    return cos, sin


def kernel(x, c, ctx, c_ctx, w_ada, b_ada, norm1_g, w_in, dn_conv, dn_a_log, dn_dt_bias, dn_norm_g,
           gla_gate_w, gla_gate_b, gla_norm_g, ret_norm_g, w_out, norm2_g, w_ff1, w_ff2, final_norm_g):
    batch, n_lat, d = x.shape
    n_ctx = ctx.shape[1]
    depth = w_in.shape[0]
    assert batch == 1 and n_ctx % BLK == 0 and n_lat % BLK == 0 and n_lat % GRID_W == 0
    lay = _layout(d)
    assert lay["d_in"] == w_in.shape[2]
    n_ctx_blk = n_ctx // BLK
    lt = n_ctx + n_lat
    h_dn, h_gla, h_ret = lay["h_dn"], lay["h_gla"], lay["h_ret"]

    w_in_b = _relayout_w_in(w_in, lay)
    s0 = lay["src"]["da_f"][0]
    ws_t = jnp.zeros((depth, LANES, d), BF16)
    ws_t = lax.dynamic_update_slice(
        ws_t, jnp.swapaxes(w_in[:, :, s0:s0 + 4 * h_dn], 1, 2).astype(BF16), (0, 0, 0))
    w_out_b = w_out.astype(BF16)
    w_ff1_b = w_ff1.astype(BF16)
    w_ff2_b = w_ff2.astype(BF16)
    gw_full = jnp.zeros((depth, 2, LANES, gla_gate_w.shape[-1]), BF16)
    for dd in range(2):
        gw_full = lax.dynamic_update_slice(
            gw_full, gla_gate_w[:, dd:dd + 1].astype(BF16), (0, dd, lay["glr_lane"][dd], 0))
    gate_b4 = gla_gate_b.reshape(depth, 2, 1, -1)
    cos_t, sin_t = _rope_tables(n_ctx, n_lat)
    log_gamma = jnp.log1p(-jnp.exp2(-5.0 - jnp.arange(h_ret, dtype=F32)))

    c2 = jnp.zeros((8, d), F32).at[0].set(c[0]).at[1].set(c_ctx)
    mod = _ada(c2, w_ada, b_ada)
    x_all = jnp.concatenate([ctx[0], x[0]], axis=0)

    for i in range(depth):
        h = _norm_mod(x_all, norm1_g[i], mod, i, 0, 1, n_ctx_blk)
        proj = _matmul(h, w_in_b[i])
        gates4 = _gates_t(ws_t[i], h).reshape(LANES, lt // BLK, BLK // CHUNK, CHUNK)
        dnp = _dn_prep(proj, dn_conv[i], h_dn, n_ctx_blk)
        dn_o = _dn_scan(dnp, gates4, dn_a_log[i], dn_dt_bias[i], h_dn, n_ctx_blk)
        gla_o = _gla_scan(proj, gw_full, gate_b4, i, lay, n_ctx_blk)
        ret_o = _ret_scan(proj, cos_t, sin_t, log_gamma, lay, n_ctx_blk)
        y = _finish(proj, dn_o, gla_o, ret_o, dn_norm_g[i], gla_norm_g[i], ret_norm_g[i], lay)
        x_all = _matmul(y, w_out_b[i], "resid", x_all, mod, i, 2, n_ctx)
        h2 = _norm_mod(x_all, norm2_g[i], mod, i, 3, 4, n_ctx_blk)
        hid = _matmul(h2, w_ff1_b[i], "relu2")
        x_all = _matmul(hid, w_ff2_b[i], "resid", x_all, mod, i, 5, n_ctx)
    return _final_norm(x_all, final_norm_g, n_ctx_blk)[None]
```

```python
import functools
import math

import numpy as np
import jax
import jax.numpy as jnp
from jax import lax
from jax.experimental import pallas as pl
from jax.experimental.pallas import tpu as pltpu

F32 = jnp.float32
BF16 = jnp.bfloat16

HEAD_DIM = 128
CHUNK = 64
GRID_W = 64
CONV_W = 5
CONV_HALO = 16
GLA_GATE_RANK = 16
GLA_GATE_NORM = 16.0
EPS = 1e-6
ROPE_BASE = 10000.0
LANES = 128
BLK = 4 * CHUNK
VMEM_LIMIT = 60000 * 1024
MM_VMEM_BUDGET = 52 * 1024 * 1024


def _cparams(sem):
    return pltpu.CompilerParams(dimension_semantics=sem, vmem_limit_bytes=VMEM_LIMIT)


def _dot(a, b):
    return jnp.dot(a, b, preferred_element_type=F32)


def _dot_nt(a, b):
    return lax.dot_general(a, b, (((1,), (1,)), ((), ())), preferred_element_type=F32)


def _dot_tn(a, b):
    return lax.dot_general(a, b, (((0,), (0,)), ((), ())), preferred_element_type=F32)


def _split3(a):
    hi = a.astype(BF16)
    r = a - hi.astype(F32)
    mid = r.astype(BF16)
    lo = (r - mid.astype(F32)).astype(BF16)
    return hi, mid, lo


def _dot_exact_rhs(a, m_bf16):
    hi, mid, lo = _split3(a)
    return _dot(hi, m_bf16) + (_dot(mid, m_bf16) + _dot(lo, m_bf16))


def _dot_exact_lhs(m_bf16, a):
    hi, mid, lo = _split3(a)
    return _dot(m_bf16, hi) + (_dot(m_bf16, mid) + _dot(m_bf16, lo))


INV_PASSES = 1


def _dot_f32(a, b):
    a_hi = a.astype(BF16)
    b_hi = b.astype(BF16)
    if INV_PASSES == 1:
        return _dot(a_hi, b_hi)
    a_lo = (a - a_hi.astype(F32)).astype(BF16)
    b_lo = (b - b_hi.astype(F32)).astype(BF16)
    return _dot(a_hi, b_hi) + (_dot(a_hi, b_lo) + _dot(a_lo, b_hi))


def _softplus(x):
    return jnp.maximum(x, 0.0) + jnp.log1p(jnp.exp(-jnp.abs(x)))


def _sigmoid(x):
    return 1.0 / (1.0 + jnp.exp(-x))


def _silu(x):
    return x * _sigmoid(x)


def _tri_masks(reverse):
    ii = lax.broadcasted_iota(jnp.int32, (CHUNK, CHUNK), 0)
    jj = lax.broadcasted_iota(jnp.int32, (CHUNK, CHUNK), 1)
    if reverse:
        return ii <= jj, ii < jj, ii == jj
    return ii >= jj, ii > jj, ii == jj


def _to_col(row, eye):
    return jnp.sum(jnp.where(eye, row, 0.0), axis=1, keepdims=True)


def _ada_kernel(s_ref, w_ref, b_ref, o_ref):
    s = _silu(s_ref[...]).astype(BF16)
    o_ref[...] = _dot(s, w_ref[...].astype(BF16)) + b_ref[...]


def _ada(c2, w_ada, b_ada):
    depth, d, n = w_ada.shape
    tn = 1024
    return pl.pallas_call(
        _ada_kernel,
        grid=(depth, n // tn),
        in_specs=[
            pl.BlockSpec((8, d), lambda l, j: (0, 0)),
            pl.BlockSpec((None, d, tn), lambda l, j: (l, 0, j)),
            pl.BlockSpec((None, 1, tn), lambda l, j: (l, 0, j)),
        ],
        out_specs=pl.BlockSpec((None, 8, tn), lambda l, j: (l, 0, j)),
        out_shape=jax.ShapeDtypeStruct((depth, 8, n), F32),
        compiler_params=_cparams(("arbitrary", "arbitrary")),
        name="ada_mod",
    )(c2, w_ada, b_ada.reshape(depth, 1, n))


def _norm_mod_kernel(x_ref, g_ref, sh_ref, sc_ref, o_ref, *, n_ctx_blk):
    is_ctx = pl.program_id(0) < n_ctx_blk
    x = x_ref[...]
    y = x * lax.rsqrt(jnp.mean(x * x, axis=-1, keepdims=True) + EPS) * g_ref[...]
    sh = jnp.where(is_ctx, sh_ref[1:2, :], sh_ref[0:1, :])
    sc = jnp.where(is_ctx, sc_ref[1:2, :], sc_ref[0:1, :])
    o_ref[...] = (y * (1.0 + sc) + sh).astype(o_ref.dtype)


def _norm_mod(x_all, gain, mod, layer, which_shift, which_scale, n_ctx_blk):
    lt, d = x_all.shape
    return pl.pallas_call(
        functools.partial(_norm_mod_kernel, n_ctx_blk=n_ctx_blk),
        grid=(lt // BLK,),
        in_specs=[
            pl.BlockSpec((BLK, d), lambda i: (i, 0)),
            pl.BlockSpec((1, d), lambda i: (0, 0)),
            pl.BlockSpec((None, 8, d), lambda i: (layer, 0, which_shift)),
            pl.BlockSpec((None, 8, d), lambda i: (layer, 0, which_scale)),
        ],
        out_specs=pl.BlockSpec((BLK, d), lambda i: (i, 0)),
        out_shape=jax.ShapeDtypeStruct((lt, d), BF16),
        compiler_params=_cparams(("arbitrary",)),
        name="norm_mod",
    )(x_all, gain.reshape(1, d), mod, mod)


def _final_norm_kernel(x_ref, g_ref, o_ref):
    x = x_ref[...]
    o_ref[...] = x * lax.rsqrt(jnp.mean(x * x, axis=-1, keepdims=True) + EPS) * g_ref[...]


def _final_norm(x_all, gain, n_ctx_blk):
    lt, d = x_all.shape
    n_lat = lt - n_ctx_blk * BLK
    return pl.pallas_call(
        _final_norm_kernel,
        grid=(n_lat // BLK,),
        in_specs=[
            pl.BlockSpec((BLK, d), lambda i: (i + n_ctx_blk, 0)),
            pl.BlockSpec((1, d), lambda i: (0, 0)),
        ],
        out_specs=pl.BlockSpec((BLK, d), lambda i: (i, 0)),
        out_shape=jax.ShapeDtypeStruct((n_lat, d), F32),
        compiler_params=_cparams(("arbitrary",)),
        name="final_norm",
    )(x_all, gain.reshape(1, d))


def _mm_epilogue(acc, mode, x_ref, gate_ref, o_ref, row0, n_ctx_rows):
    if mode == "plain":
        o_ref[...] = acc.astype(o_ref.dtype)
    elif mode == "relu2":
        r = jnp.maximum(acc, 0.0)
        o_ref[...] = (r * r).astype(o_ref.dtype)
    else:
        row = row0 + lax.broadcasted_iota(jnp.int32, (acc.shape[0], 1), 0)
        gate = jnp.where(row < n_ctx_rows, gate_ref[1:2, :], gate_ref[0:1, :])
        o_ref[...] = x_ref[...] + gate * acc


def _mm_kernel(*refs, mode, nk, tm, n_ctx_rows):
    if mode == "resid":
        a_ref, b_ref, x_ref, gate_ref = refs[:4]
        rest = refs[4:]
    else:
        a_ref, b_ref = refs[:2]
        x_ref = gate_ref = None
        rest = refs[2:]
    o_ref = rest[0]
    row0 = pl.program_id(1) * tm
    if nk == 1:
        acc = _dot(a_ref[...], b_ref[...])
        _mm_epilogue(acc, mode, x_ref, gate_ref, o_ref, row0, n_ctx_rows)
        return
    acc_ref = rest[1]
    k = pl.program_id(2)

    @pl.when(k == 0)
    def _():
        acc_ref[...] = jnp.zeros_like(acc_ref)

    acc_ref[...] += _dot(a_ref[...], b_ref[...])

    @pl.when(k == nk - 1)
    def _():
        _mm_epilogue(acc_ref[...], mode, x_ref, gate_ref, o_ref, row0, n_ctx_rows)


def _mm_tiles(m, n, k, mode):
    out_bytes = 4 if mode == "resid" else 2
    for tm in (1280, 1024, 768, 512, 256):
        if m % tm:
            continue
        for tn, tk in ((1024, 4096), (768, 4096), (1024, 2048), (512, 4096), (512, 2048), (512, 1024),
                       (256, 1024)):
            if n % tn or k % min(tk, k):
                continue
            tk = min(tk, k)
            use = 2 * (tm * tk * 2 + tk * tn * 2 + tm * tn * out_bytes) + tm * tn * 4
            if mode == "resid":
                use += 2 * tm * tn * 4
            if k > tk:
                use += tm * tn * 4
            if use <= MM_VMEM_BUDGET:
                return tm, tn, tk
    raise ValueError(f"no matmul tiling for {(m, n, k)}")


def _matmul(a, b, layer, mode="plain", x=None, mod=None, which_gate=0, n_ctx_rows=0):
    m, k = a.shape
    n = b.shape[2]
    tm, tn, tk = _mm_tiles(m, n, k, mode)
    nk = k // tk
    in_specs = [
        pl.BlockSpec((tm, tk), lambda j, i, kk: (i, kk)),
        pl.BlockSpec((None, tk, tn), lambda j, i, kk: (layer, kk, j)),
    ]
    args = [a, b]
    if mode == "resid":
        gate_blk0 = which_gate * (n // tn)
        in_specs += [
            pl.BlockSpec((tm, tn), lambda j, i, kk: (i, j)),
            pl.BlockSpec((None, 8, tn), lambda j, i, kk: (layer, 0, gate_blk0 + j)),
        ]
        args += [x, mod]
    scratch = [pltpu.VMEM((tm, tn), F32)] if nk > 1 else []
    return pl.pallas_call(
        functools.partial(_mm_kernel, mode=mode, nk=nk, tm=tm, n_ctx_rows=n_ctx_rows),
        grid=(n // tn, m // tm, nk),
        in_specs=in_specs,
        out_specs=pl.BlockSpec((tm, tn), lambda j, i, kk: (i, j)),
        out_shape=jax.ShapeDtypeStruct((m, n), F32 if mode == "resid" else BF16),
        scratch_shapes=scratch,
        compiler_params=_cparams(("arbitrary", "arbitrary", "arbitrary")),
        name="mm_" + mode,
    )(*args)


def _gates_t_kernel(w_ref, h_ref, o_ref):
    o_ref[...] = _dot_nt(w_ref[...], h_ref[...])


def _gates_t(ws_t, layer, h):
    lt, d = h.shape
    tm = 1280 if lt % 1280 == 0 else BLK
    return pl.pallas_call(
        _gates_t_kernel,
        grid=(lt // tm,),
        in_specs=[
            pl.BlockSpec((None, LANES, d), lambda i: (layer, 0, 0)),
            pl.BlockSpec((tm, d), lambda i: (i, 0)),
        ],
        out_specs=pl.BlockSpec((LANES, tm), lambda i: (0, i)),
        out_shape=jax.ShapeDtypeStruct((LANES, lt), F32),
        compiler_params=_cparams(("arbitrary",)),
        name="gates_t",
    )(ws_t, h)


def _dn_prep_kernel(prev_ref, main_ref, next_ref, w_ref, o_ref, ext_ref, *, n_ctx_blk, n_blk, n_heads):
    i = pl.program_id(0)
    prev_ok = jnp.logical_and(i != 0, i != n_ctx_blk)
    next_ok = jnp.logical_and(i != n_ctx_blk - 1, i != n_blk - 1)
    ext_ref[0:CONV_HALO, :] = jnp.where(prev_ok, prev_ref[...].astype(F32), 0.0)
    ext_ref[CONV_HALO:CONV_HALO + BLK, :] = main_ref[...].astype(F32)
    ext_ref[CONV_HALO + BLK:, :] = jnp.where(next_ok, next_ref[...].astype(F32), 0.0)
    for g in range(3 * n_heads):
        cs = slice(g * HEAD_DIM, (g + 1) * HEAD_DIM)
        acc = None
        for tap in range(CONV_W):
            win = ext_ref[pl.ds(CONV_HALO - CONV_W // 2 + tap, BLK), cs]
            term = win * w_ref[tap:tap + 1, cs]
            acc = term if acc is None else acc + term
        y = _silu(acc)
        if g < 2 * n_heads:
            y = y * lax.rsqrt(jnp.sum(y * y, axis=-1, keepdims=True) + EPS)
            if g < n_heads:
                y = y * HEAD_DIM ** -0.5
        o_ref[:, cs] = y.astype(o_ref.dtype)


def _dn_prep(proj, conv_w, n_heads, n_ctx_blk):
    lt = proj.shape[0]
    c = 3 * n_heads * HEAD_DIM
    n_blk = lt // BLK
    per = BLK // CONV_HALO
    n_halo = lt // CONV_HALO
    return pl.pallas_call(
        functools.partial(_dn_prep_kernel, n_ctx_blk=n_ctx_blk, n_blk=n_blk, n_heads=n_heads),
        grid=(n_blk,),
        in_specs=[
            pl.BlockSpec((CONV_HALO, c), lambda i: (jnp.maximum(i * per - 1, 0), 0)),
            pl.BlockSpec((BLK, c), lambda i: (i, 0)),
            pl.BlockSpec((CONV_HALO, c), lambda i: (jnp.minimum((i + 1) * per, n_halo - 1), 0)),
            pl.BlockSpec((CONV_W, c), lambda i: (0, 0)),
        ],
        out_specs=pl.BlockSpec((BLK, c), lambda i: (i, 0)),
        out_shape=jax.ShapeDtypeStruct((lt, c), BF16),
        scratch_shapes=[pltpu.VMEM((BLK + 2 * CONV_HALO, c), F32)],
        compiler_params=_cparams(("arbitrary",)),
        name="dn_prep",
    )(proj, proj, proj, conv_w)


def _fwd_blk(s):
    return s


def _bwd_blk(s, n_ctx_blk, n_blk):
    return jnp.where(s < n_ctx_blk, n_ctx_blk - 1 - s, n_blk - 1 - (s - n_ctx_blk))


INV_BASE = 8
DN_HEADS_PER_STEP = 4
RET_HEADS_PER_STEP = 5


def _unit_triangular_inverses(a_list):
    ii = lax.broadcasted_iota(jnp.int32, (CHUNK, CHUNK), 0)
    jj = lax.broadcasted_iota(jnp.int32, (CHUNK, CHUNK), 1)
    diag_blk = ii // INV_BASE == jj // INV_BASE
    eye_f = jnp.where(ii == jj, 1.0, 0.0)
    d = [jnp.where(diag_blk, a, 0.0) for a in a_list]
    m = [eye_f - x for x in d]
    p = [_dot_f32(x, x) for x in d]
    size = 2
    while size < INV_BASE:
        m = [mi + _dot_f32(mi, pi) for mi, pi in zip(m, p)]
        size *= 2
        if size < INV_BASE:
            p = [_dot_f32(pi, pi) for pi in p]
    size = INV_BASE
    while size < CHUNK:
        pair = jnp.logical_and(ii // (2 * size) == jj // (2 * size), ii // size != jj // size)
        t = [_dot_f32(mi, jnp.where(pair, a, 0.0)) for mi, a in zip(m, a_list)]
        m = [mi - _dot_f32(ti, mi) for mi, ti in zip(m, t)]
        size *= 2
    return m


def _alternate(per_dir):
    return [it for pair in zip(*per_dir) for it in pair]


def _chunk_order(reverse):
    n_chunks = BLK // CHUNK
    return range(n_chunks - 1, -1, -1) if reverse else range(n_chunks)


def _dn_scan_kernel(alog_ref, dtb_ref, qf, kf, vf, daf, dbf, qb, kb, vb, dab, dbb, of_ref, ob_ref, s_ref,
                    *, hg):
    @pl.when(pl.program_id(1) == 0)
    def _():
        s_ref[...] = jnp.zeros_like(s_ref)

    ii = lax.broadcasted_iota(jnp.int32, (CHUNK, CHUNK), 0)
    jj = lax.broadcasted_iota(jnp.int32, (CHUNK, CHUNK), 1)
    o_refs = (of_ref, ob_ref)
    chains = []
    for d, (q_ref, k_ref, v_ref, da_ref, db_ref) in enumerate(((qf, kf, vf, daf, dbf), (qb, kb, vb, dab, dbb))):
        reverse = d == 1
        incl, strict, eye = _tri_masks(reverse)
        cum = jnp.where((ii >= jj) if reverse else (ii <= jj), 1.0, 0.0).astype(BF16)
        last = 0 if reverse else CHUNK - 1
        for j in range(hg):
            h = pl.program_id(0) * hg + j
            cols = slice(j * HEAD_DIM, (j + 1) * HEAD_DIM)
            neg_a = -jnp.exp(jnp.full((1, CHUNK), alog_ref[d, h], F32))
            g = neg_a * _softplus(da_ref[j] + dtb_ref[d, h])
            beta = _sigmoid(db_ref[j])
            gc_rows = _dot_exact_rhs(g, cum)
            items = []
            for c in _chunk_order(reverse):
                rows = slice(c * CHUNK, (c + 1) * CHUNK)
                gc_r = gc_rows[c:c + 1, :]
                items.append(dict(
                    d=d, chain=d * hg + j, rows=rows, cols=cols, incl=incl, strict=strict, gc_r=gc_r,
                    gc_c=_to_col(gc_r, eye), b_c=_to_col(beta[c:c + 1, :], eye),
                    g_last=gc_r[:, last:last + 1],
                    q=q_ref[rows, cols], k=k_ref[rows, cols], v=v_ref[rows, cols]))
            chains.append(items)
    items = _alternate(chains)

    for it in items:
        it["kk"] = _dot_nt(it["k"], it["k"])
        it["qk"] = _dot_nt(it["q"], it["k"])
    for it in items:
        decay = jnp.exp(jnp.where(it["incl"], it["gc_c"] - it["gc_r"], -jnp.inf))
        it["a"] = jnp.where(it["strict"], it["kk"] * decay, 0.0) * it["b_c"]
        it["qk"] = (it["qk"] * decay).astype(BF16)
        e_gc = jnp.exp(it["gc_c"])
        kf32 = it["k"].astype(F32)
        it["rhs"] = jnp.concatenate(
            [(kf32 * (it["b_c"] * e_gc)).astype(BF16), (it["v"].astype(F32) * it["b_c"]).astype(BF16)], axis=1)
        it["q_dec"] = it["q"].astype(F32) * e_gc
        it["k_dec"] = (kf32 * jnp.exp(it["g_last"] - it["gc_c"])).astype(BF16)
    t_inv = _unit_triangular_inverses([it["a"] for it in items])
    for it, t in zip(items, t_inv):
        it["wu"] = _dot(t.astype(BF16), it["rhs"]).astype(BF16)
    for it in items:
        it["qkwu"] = _dot(it["qk"], it["wu"])
        it["kdwu"] = _dot_tn(it["k_dec"], it["wu"])
    for it in items:
        it["q2"] = (it["q_dec"] - it["qkwu"][:, :HEAD_DIM]).astype(BF16)
        it["m1"] = it["kdwu"][:, :HEAD_DIM].astype(BF16)

    s = [s_ref[n] for n in range(2 * hg)]
    for it in items:
        n = it["chain"]
        s_b = s[n].astype(BF16)
        o = _dot(it["q2"], s_b) + it["qkwu"][:, HEAD_DIM:]
        o_ref = o_refs[it["d"]]
        o_ref[it["rows"], it["cols"]] = o.astype(o_ref.dtype)
        s[n] = jnp.exp(it["g_last"]) * s[n] - _dot(it["m1"], s_b) + it["kdwu"][:, HEAD_DIM:]
    for n in range(2 * hg):
        s_ref[n] = s[n]


def _dn_scan(dnp, gates4, a_log, dt_bias, n_heads, n_ctx_blk):
    lt = dnp.shape[0]
    n_blk = lt // BLK
    bwd = functools.partial(_bwd_blk, n_ctx_blk=n_ctx_blk, n_blk=n_blk)
    n_chunks = BLK // CHUNK
    hg = DN_HEADS_PER_STEP
    assert n_heads % hg == 0
    n_grp = n_heads // hg
    width = hg * HEAD_DIM

    def col(grp_off, blk_fn):
        return pl.BlockSpec((BLK, width), lambda g, s: (blk_fn(s), grp_off + g))

    def gate(grp_off, blk_fn):
        return pl.BlockSpec((hg, None, n_chunks, CHUNK), lambda g, s: (grp_off + g, blk_fn(s), 0, 0))

    smem = pl.BlockSpec(memory_space=pltpu.SMEM)
    out_sds = jax.ShapeDtypeStruct((lt, n_heads * HEAD_DIM), BF16)
    return pl.pallas_call(
        functools.partial(_dn_scan_kernel, hg=hg),
        grid=(n_grp, n_blk),
        in_specs=[
            smem, smem,
            col(0, _fwd_blk), col(n_grp, _fwd_blk), col(2 * n_grp, _fwd_blk),
            gate(0, _fwd_blk), gate(2 * n_grp, _fwd_blk),
            col(0, bwd), col(n_grp, bwd), col(2 * n_grp, bwd),
            gate(n_grp, bwd), gate(3 * n_grp, bwd),
        ],
        out_specs=[
            pl.BlockSpec((BLK, width), lambda g, s: (s, g)),
            pl.BlockSpec((BLK, width), lambda g, s: (bwd(s), g)),
        ],
        out_shape=[out_sds, out_sds],
        scratch_shapes=[pltpu.VMEM((2 * hg, HEAD_DIM, HEAD_DIM), F32)],
        compiler_params=_cparams(("arbitrary", "arbitrary")),
        name="dn_scan",
    )(a_log, dt_bias, dnp, dnp, dnp, gates4, gates4, dnp, dnp, dnp, gates4, gates4)


def _gla_scan_kernel(qf, kf, vf, smf, gwf, gbf, qb, kb, vb, smb, gwb, gbb, of_ref, ob_ref, st_ref, *, dk, pg):
    @pl.when(pl.program_id(1) == 0)
    def _():
        st_ref[...] = jnp.zeros_like(st_ref)

    ii = lax.broadcasted_iota(jnp.int32, (CHUNK, CHUNK), 0)
    jj = lax.broadcasted_iota(jnp.int32, (CHUNK, CHUNK), 1)
    lane = lax.broadcasted_iota(jnp.int32, (1, LANES), 1)
    n_sub = LANES // dk
    sels = [(lane // dk) == sub for sub in range(n_sub)]
    o_refs = (of_ref, ob_ref)
    chains = []
    for d, (q_ref, k_ref, v_ref, sm_ref, gw_ref, gb_ref) in enumerate(
            ((qf, kf, vf, smf, gwf, gbf), (qb, kb, vb, smb, gwb, gbb))):
        reverse = d == 1
        incl, _, _ = _tri_masks(reverse)
        cum = jnp.where((ii <= jj) if reverse else (ii >= jj), 1.0, 0.0).astype(BF16)
        z = _dot(sm_ref[...], gw_ref[...]) + gb_ref[...]
        log_a = (jnp.minimum(z, 0.0) - jnp.log1p(jnp.exp(-jnp.abs(z)))) * (1.0 / GLA_GATE_NORM)
        last = 0 if reverse else CHUNK - 1
        for j in range(pg):
            cols = slice(j * LANES, (j + 1) * LANES)
            vcols = [slice((j * n_sub + sub) * HEAD_DIM, (j * n_sub + sub + 1) * HEAD_DIM) for sub in range(n_sub)]
            items = []
            for c in _chunk_order(reverse):
                rows = slice(c * CHUNK, (c + 1) * CHUNK)
                items.append(dict(d=d, chain=d * pg + j, rows=rows, vcols=vcols, incl=incl, cum=cum, last=last,
                                  log_a=log_a[rows, cols], q=q_ref[rows, cols], k=k_ref[rows, cols],
                                  v=[v_ref[rows, vc] for vc in vcols]))
            chains.append(items)
    items = _alternate(chains)

    for it in items:
        it["gc"] = _dot_exact_lhs(it["cum"], it["log_a"])
    for it in items:
        gc = it["gc"]
        g_last = gc[it["last"]:it["last"] + 1, :]
        kf32 = it["k"].astype(F32)
        q_in = it["q"].astype(F32) * jnp.exp(gc) * dk ** -0.5
        it["qm"] = [jnp.where(sel, q_in, 0.0).astype(BF16) for sel in sels]
        it["k_in"] = (kf32 * jnp.exp(-gc)).astype(BF16)
        it["k_dec"] = (kf32 * jnp.exp(g_last - gc)).astype(BF16)
        it["decay"] = jnp.exp(g_last)
    for it in items:
        it["scores"] = [jnp.where(it["incl"], _dot_nt(qm, it["k_in"]), 0.0).astype(BF16) for qm in it["qm"]]
        upd = None
        for sel, v in zip(sels, it["v"]):
            u = _dot_tn(v, it["k_dec"])
            upd = u if upd is None else jnp.where(sel, u, upd)
        it["upd"] = upd
    for it in items:
        it["intra"] = [_dot(sc, v) for sc, v in zip(it["scores"], it["v"])]

    st = [st_ref[n] for n in range(2 * pg)]
    for it in items:
        n = it["chain"]
        st_b = st[n].astype(BF16)
        o_ref = o_refs[it["d"]]
        for sub in range(n_sub):
            o = it["intra"][sub] + _dot_nt(it["qm"][sub], st_b)
            o_ref[it["rows"], it["vcols"][sub]] = o.astype(o_ref.dtype)
        st[n] = st[n] * it["decay"] + it["upd"]
    for n in range(2 * pg):
        st_ref[n] = st[n]


def _gla_scan(proj, gw_full, gate_b, layer, lay, n_ctx_blk):
    lt = proj.shape[0]
    n_blk = lt // BLK
    n_heads, dk = lay["h_gla"], lay["gla_dk"]
    n_sub = LANES // dk
    pg = n_heads // n_sub
    qw, vw = pg * LANES, pg * n_sub * HEAD_DIM
    assert lay["gq"] % qw == 0 and lay["gk"] % qw == 0 and lay["gv"] % vw == 0
    bwd = functools.partial(_bwd_blk, n_ctx_blk=n_ctx_blk, n_blk=n_blk)
    q0, k0, v0, sm0 = lay["gq"] // qw, lay["gk"] // qw, lay["gv"] // vw, lay["small"] // LANES

    def specs(d, blk_fn):
        return [
            pl.BlockSpec((BLK, qw), lambda p, s: (blk_fn(s), q0 + p)),
            pl.BlockSpec((BLK, qw), lambda p, s: (blk_fn(s), k0 + p)),
            pl.BlockSpec((BLK, vw), lambda p, s: (blk_fn(s), v0 + p)),
            pl.BlockSpec((BLK, LANES), lambda p, s: (blk_fn(s), sm0)),
            pl.BlockSpec((None, None, LANES, qw), lambda p, s: (layer, d, 0, p)),
            pl.BlockSpec((None, None, 1, qw), lambda p, s: (layer, d, 0, p)),
        ]

    out_sds = jax.ShapeDtypeStruct((lt, n_heads * HEAD_DIM), BF16)
    return pl.pallas_call(
        functools.partial(_gla_scan_kernel, dk=dk, pg=pg),
        grid=(n_heads // (n_sub * pg), n_blk),
        in_specs=specs(0, _fwd_blk) + specs(1, bwd),
        out_specs=[
            pl.BlockSpec((BLK, vw), lambda p, s: (s, p)),
            pl.BlockSpec((BLK, vw), lambda p, s: (bwd(s), p)),
        ],
        out_shape=[out_sds, out_sds],
        scratch_shapes=[pltpu.VMEM((2 * pg, HEAD_DIM, LANES), F32)],
        compiler_params=_cparams(("arbitrary", "arbitrary")),
        name="gla_scan",
    )(proj, proj, proj, proj, gw_full, gate_b, proj, proj, proj, proj, gw_full, gate_b)


def _rope(t, cos, sin_signed, first_half):
    partner = jnp.where(first_half, pltpu.roll(t, LANES - HEAD_DIM // 4, 1), pltpu.roll(t, HEAD_DIM // 4, 1))
    return t * cos + partner * sin_signed


def _ret_scan_kernel(lg_ref, qf, kf, vf, cf, sf, qb, kb, vb, cb, sb, of_ref, ob_ref, s_ref, *, hg):
    @pl.when(pl.program_id(1) == 0)
    def _():
        s_ref[...] = jnp.zeros_like(s_ref)

    ii = lax.broadcasted_iota(jnp.int32, (CHUNK, CHUNK), 0)
    jj = lax.broadcasted_iota(jnp.int32, (CHUNK, CHUNK), 1)
    pos = lax.broadcasted_iota(jnp.int32, (CHUNK, 1), 0).astype(F32)
    lane = lax.broadcasted_iota(jnp.int32, (1, LANES), 1)
    first_half = (lane % (HEAD_DIM // 2)) < (HEAD_DIM // 4)
    o_refs = (of_ref, ob_ref)
    lgs = [lg_ref[pl.program_id(0) * hg + j] for j in range(hg)]
    g_chunk = [jnp.exp(jnp.full((1, LANES), lg * CHUNK, F32)) for lg in lgs]
    chains = []
    for d, (q_ref, k_ref, v_ref, cos_ref, sin_ref) in enumerate(((qf, kf, vf, cf, sf), (qb, kb, vb, cb, sb))):
        reverse = d == 1
        incl, _, _ = _tri_masks(reverse)
        dist = ((jj - ii) if reverse else (ii - jj)).astype(F32)
        tables = [(cos_ref[slice(c * CHUNK, (c + 1) * CHUNK), :], sin_ref[slice(c * CHUNK, (c + 1) * CHUNK), :])
                  for c in range(BLK // CHUNK)]
        for j, lg in enumerate(lgs):
            cols = slice(j * HEAD_DIM, (j + 1) * HEAD_DIM)
            decay = jnp.exp(jnp.where(incl, lg * dist, -jnp.inf))
            if reverse:
                q_scale = jnp.exp(lg * (CHUNK - pos))
                k_scale = jnp.exp(lg * pos)
            else:
                q_scale = jnp.exp(lg * (pos + 1.0))
                k_scale = jnp.exp(lg * (CHUNK - 1.0 - pos))
            items = []
            for c in _chunk_order(reverse):
                rows = slice(c * CHUNK, (c + 1) * CHUNK)
                cos, sin = tables[c]
                q = _rope(q_ref[rows, cols].astype(F32) * HEAD_DIM ** -0.5, cos, sin, first_half)
                k = _rope(k_ref[rows, cols].astype(F32), cos, sin, first_half)
                items.append(dict(d=d, chain=d * hg + j, head=j, rows=rows, cols=cols, decay=decay,
                                  q=q.astype(BF16), k=k.astype(BF16),
                                  q_dec=(q * q_scale).astype(BF16), k_dec=(k * k_scale).astype(BF16),
                                  v=v_ref[rows, cols]))
            chains.append(items)
    items = _alternate(chains)

    for it in items:
        it["scores"] = (_dot_nt(it["q"], it["k"]) * it["decay"]).astype(BF16)
        it["upd"] = _dot_tn(it["k_dec"], it["v"])
    for it in items:
        it["intra"] = _dot(it["scores"], it["v"])

    s = [s_ref[n] for n in range(2 * hg)]
    for it in items:
        n = it["chain"]
        o = it["intra"] + _dot(it["q_dec"], s[n].astype(BF16))
        o_ref = o_refs[it["d"]]
        o_ref[it["rows"], it["cols"]] = o.astype(o_ref.dtype)
        s[n] = g_chunk[it["head"]] * s[n] + it["upd"]
    for n in range(2 * hg):
        s_ref[n] = s[n]


def _ret_scan(proj, cos_t, sin_t, log_gamma, lay, n_ctx_blk):
    lt = proj.shape[0]
    n_blk = lt // BLK
    n_heads = lay["h_ret"]
    hg = RET_HEADS_PER_STEP
    width = hg * HEAD_DIM
    assert n_heads % hg == 0 and lay["rq"] % width == 0 and lay["rk"] % width == 0 and lay["rv"] % width == 0
    bwd = functools.partial(_bwd_blk, n_ctx_blk=n_ctx_blk, n_blk=n_blk)
    q0, k0, v0 = lay["rq"] // width, lay["rk"] // width, lay["rv"] // width

    def specs(blk_fn):
        return [
            pl.BlockSpec((BLK, width), lambda g, s: (blk_fn(s), q0 + g)),
            pl.BlockSpec((BLK, width), lambda g, s: (blk_fn(s), k0 + g)),
            pl.BlockSpec((BLK, width), lambda g, s: (blk_fn(s), v0 + g)),
            pl.BlockSpec((BLK, HEAD_DIM), lambda g, s: (blk_fn(s), 0)),
            pl.BlockSpec((BLK, HEAD_DIM), lambda g, s: (blk_fn(s), 0)),
        ]

    out_sds = jax.ShapeDtypeStruct((lt, n_heads * HEAD_DIM), BF16)
    return pl.pallas_call(
        functools.partial(_ret_scan_kernel, hg=hg),
        grid=(n_heads // hg, n_blk),
        in_specs=[pl.BlockSpec(memory_space=pltpu.SMEM)] + specs(_fwd_blk) + specs(bwd),
        out_specs=[
            pl.BlockSpec((BLK, width), lambda g, s: (s, g)),
            pl.BlockSpec((BLK, width), lambda g, s: (bwd(s), g)),
        ],
        out_shape=[out_sds, out_sds],
        scratch_shapes=[pltpu.VMEM((2 * hg, HEAD_DIM, HEAD_DIM), F32)],
        compiler_params=_cparams(("arbitrary", "arbitrary")),
        name="ret_scan",
    )(log_gamma, proj, proj, proj, cos_t, sin_t, proj, proj, proj, cos_t, sin_t)


def _finish_kernel(dnf, dnb, dz, glf, glb, gg, rtf, rtb, rg, g_dn, g_gla, g_ret, o_ref, *, h_dn, h_gla, h_ret):
    col = 0
    for (of, ob, gate, gain, n_heads, centred) in (
        (dnf, dnb, dz, g_dn, h_dn, False),
        (glf, glb, gg, g_gla, h_gla, False),
        (rtf, rtb, rg, g_ret, h_ret, True),
    ):
        for h in range(n_heads):
            cs = slice(h * HEAD_DIM, (h + 1) * HEAD_DIM)
            o = of[:, cs].astype(F32) + ob[:, cs].astype(F32)
            if centred:
                o = o - jnp.mean(o, axis=-1, keepdims=True)
            y = o * lax.rsqrt(jnp.mean(o * o, axis=-1, keepdims=True) + EPS) * gain[...]
            y = y * _silu(gate[:, cs].astype(F32))
            o_ref[:, col:col + HEAD_DIM] = y.astype(o_ref.dtype)
            col += HEAD_DIM


def _finish(proj, dn_o, gla_o, ret_o, g_dn, g_gla, g_ret, lay):
    lt = proj.shape[0]
    h_dn, h_gla, h_ret = lay["h_dn"], lay["h_gla"], lay["h_ret"]
    w_dn, w_gla, w_ret = h_dn * HEAD_DIM, h_gla * HEAD_DIM, h_ret * HEAD_DIM
    d_mix = w_dn + w_gla + w_ret

    def rows(w, cblk=0):
        return pl.BlockSpec((BLK, w), lambda i: (i, cblk))

    gain = pl.BlockSpec((1, HEAD_DIM), lambda i: (0, 0))
    return pl.pallas_call(
        functools.partial(_finish_kernel, h_dn=h_dn, h_gla=h_gla, h_ret=h_ret),
        grid=(lt // BLK,),
        in_specs=[
            rows(w_dn), rows(w_dn), rows(w_dn, lay["dz"] // w_dn),
            rows(w_gla), rows(w_gla), rows(w_gla, lay["gg"] // w_gla),
            rows(w_ret), rows(w_ret), rows(w_ret, lay["rg"] // w_ret),
            gain, gain, gain,
        ],
        out_specs=rows(d_mix),
        out_shape=jax.ShapeDtypeStruct((lt, d_mix), BF16),
        compiler_params=_cparams(("arbitrary",)),
        name="finish",
    )(dn_o[0], dn_o[1], proj, gla_o[0], gla_o[1], proj, ret_o[0], ret_o[1], proj,
      g_dn.reshape(1, HEAD_DIM), g_gla.reshape(1, HEAD_DIM), g_ret.reshape(1, HEAD_DIM))


def _layout(d_model):
    n_heads = d_model // HEAD_DIM
    h_dn = (3 * d_model) // (8 * HEAD_DIM)
    h_gla = (5 * d_model) // (16 * HEAD_DIM)
    h_ret = n_heads - h_dn - h_gla
    gla_dk = HEAD_DIM // 2
    d_dn, d_gk, d_gv, d_ret = h_dn * HEAD_DIM, h_gla * gla_dk, h_gla * HEAD_DIM, h_ret * HEAD_DIM
    sizes = dict(dq=d_dn, dk=d_dn, dv=d_dn, dz=d_dn, da_f=h_dn, da_b=h_dn, db_f=h_dn, db_b=h_dn,
                 gq=d_gk, gk=d_gk, gv=d_gv, gg=d_gv, glr_f=GLA_GATE_RANK, glr_b=GLA_GATE_RANK,
                 rq=d_ret, rk=d_ret, rv=d_ret, rg=d_ret)
    src, off = {}, 0
    for name, size in sizes.items():
        src[name] = (off, size)
        off += size
    assert d_gv == d_ret == 2 * d_gk and 4 * h_dn + 2 * GLA_GATE_RANK <= LANES
    lay = dict(h_dn=h_dn, h_gla=h_gla, h_ret=h_ret, gla_dk=gla_dk, src=src, d_in=off)
    pos = 0
    order = []
    for name in ("dq", "dk", "dv", "dz"):
        lay[name] = pos
        order.append((name, pos))
        pos += d_dn
    lay["small"] = pos
    small_names = ("da_f", "da_b", "db_f", "db_b", "glr_f", "glr_b")
    sp = pos
    for name in small_names:
        lay[name] = sp
        order.append((name, sp))
        sp += sizes[name]
    pos += LANES
    pos = -(-pos // d_gv) * d_gv
    for name in ("gq", "gk"):
        lay[name] = pos
        order.append((name, pos))
        pos += d_gk
    for name in ("gv", "gg", "rq", "rk", "rv", "rg"):
        lay[name] = pos
        order.append((name, pos))
        pos += d_gv
    lay["n_cols"] = pos
    lay["order"] = order
    lay["glr_lane"] = (lay["glr_f"] - lay["small"], lay["glr_b"] - lay["small"])
    return lay


def _relayout_w_in(w_in, lay):
    depth, d, _ = w_in.shape
    out = jnp.zeros((depth, d, lay["n_cols"]), BF16)
    for name, pos in lay["order"]:
        s0, size = lay["src"][name]
        out = lax.dynamic_update_slice(out, w_in[:, :, s0:s0 + size].astype(BF16), (0, 0, pos))
    return out


def _rope_tables(n_ctx, n_lat):
    quarter = HEAD_DIM // 4
    inv = ROPE_BASE ** (-jnp.arange(quarter, dtype=F32) / quarter)
    t = jnp.arange(n_lat, dtype=jnp.int32)
    row = (t // GRID_W).astype(F32)[:, None] * inv
    colp = (t % GRID_W).astype(F32)[:, None] * inv
    cos = jnp.concatenate([jnp.cos(row), jnp.cos(row), jnp.cos(colp), jnp.cos(colp)], axis=-1)
    sin = jnp.concatenate([-jnp.sin(row), jnp.sin(row), -jnp.sin(colp), jnp.sin(colp)], axis=-1)
    cos = jnp.concatenate([jnp.ones((n_ctx, HEAD_DIM), F32), cos], axis=0)
    sin = jnp.concatenate([jnp.zeros((n_ctx, HEAD_DIM), F32), sin], axis=0)
    return cos, sin


def kernel(x, c, ctx, c_ctx, w_ada, b_ada, norm1_g, w_in, dn_conv, dn_a_log, dn_dt_bias, dn_norm_g,
           gla_gate_w, gla_gate_b, gla_norm_g, ret_norm_g, w_out, norm2_g, w_ff1, w_ff2, final_norm_g):
    batch, n_lat, d = x.shape
    n_ctx = ctx.shape[1]
    depth = w_in.shape[0]
    assert batch == 1 and n_ctx % BLK == 0 and n_lat % BLK == 0 and n_lat % GRID_W == 0
    lay = _layout(d)
    assert lay["d_in"] == w_in.shape[2]
    n_ctx_blk = n_ctx // BLK
    lt = n_ctx + n_lat
    h_dn, h_gla, h_ret = lay["h_dn"], lay["h_gla"], lay["h_ret"]

    w_in_b = _relayout_w_in(w_in, lay)
    s0 = lay["src"]["da_f"][0]
    ws_t = jnp.zeros((depth, LANES, d), BF16)
    ws_t = lax.dynamic_update_slice(
        ws_t, jnp.swapaxes(w_in[:, :, s0:s0 + 4 * h_dn], 1, 2).astype(BF16), (0, 0, 0))
    w_out_b = w_out.astype(BF16)
    w_ff1_b = w_ff1.astype(BF16)
    w_ff2_b = w_ff2.astype(BF16)
    gw_full = jnp.zeros((depth, 2, LANES, gla_gate_w.shape[-1]), BF16)
    for dd in range(2):
        gw_full = lax.dynamic_update_slice(
            gw_full, gla_gate_w[:, dd:dd + 1].astype(BF16), (0, dd, lay["glr_lane"][dd], 0))
    gate_b4 = gla_gate_b.reshape(depth, 2, 1, -1)
    cos_t, sin_t = _rope_tables(n_ctx, n_lat)
    log_gamma = jnp.log1p(-jnp.exp2(-5.0 - jnp.arange(h_ret, dtype=F32)))

    c2 = jnp.zeros((8, d), F32).at[0].set(c[0]).at[1].set(c_ctx)
    mod = _ada(c2, w_ada, b_ada)
    x_all = jnp.concatenate([ctx[0], x[0]], axis=0)

    for i in range(depth):
        h = _norm_mod(x_all, norm1_g[i], mod, i, 0, 1, n_ctx_blk)
        proj = _matmul(h, w_in_b, i)
        gates4 = _gates_t(ws_t, i, h).reshape(LANES, lt // BLK, BLK // CHUNK, CHUNK)
        dnp = _dn_prep(proj, dn_conv[i], h_dn, n_ctx_blk)
        dn_o = _dn_scan(dnp, gates4, dn_a_log[i], dn_dt_bias[i], h_dn, n_ctx_blk)
        gla_o = _gla_scan(proj, gw_full, gate_b4, i, lay, n_ctx_blk)
        ret_o = _ret_scan(proj, cos_t, sin_t, log_gamma, lay, n_ctx_blk)
        y = _finish(proj, dn_o, gla_o, ret_o, dn_norm_g[i], gla_norm_g[i], ret_norm_g[i], lay)
        x_all = _matmul(y, w_out_b, i, "resid", x_all, mod, 2, n_ctx)
        h2 = _norm_mod(x_all, norm2_g[i], mod, i, 3, 4, n_ctx_blk)
        hid = _matmul(h2, w_ff1_b, i, "relu2")
        x_all = _matmul(hid, w_ff2_b, i, "resid", x_all, mod, 5, n_ctx)
    return _final_norm(x_all, final_norm_g, n_ctx_blk)[None]
```

```python
import functools
import math

import numpy as np
import jax
import jax.numpy as jnp
from jax import lax
from jax.experimental import pallas as pl
from jax.experimental.pallas import tpu as pltpu

F32 = jnp.float32
BF16 = jnp.bfloat16

HEAD_DIM = 128
CHUNK = 64
GRID_W = 64
CONV_W = 5
CONV_HALO = 16
GLA_GATE_RANK = 16
GLA_GATE_NORM = 16.0
EPS = 1e-6
ROPE_BASE = 10000.0
LANES = 128
BLK = 4 * CHUNK
VMEM_LIMIT = 60000 * 1024
MM_VMEM_BUDGET = 52 * 1024 * 1024


def _cparams(sem):
    return pltpu.CompilerParams(dimension_semantics=sem, vmem_limit_bytes=VMEM_LIMIT)


def _dot(a, b):
    return jnp.dot(a, b, preferred_element_type=F32)


def _dot_nt(a, b):
    return lax.dot_general(a, b, (((1,), (1,)), ((), ())), preferred_element_type=F32)


def _dot_tn(a, b):
    return lax.dot_general(a, b, (((0,), (0,)), ((), ())), preferred_element_type=F32)


def _split3(a):
    hi = a.astype(BF16)
    r = a - hi.astype(F32)
    mid = r.astype(BF16)
    lo = (r - mid.astype(F32)).astype(BF16)
    return hi, mid, lo


def _dot_exact_rhs(a, m_bf16):
    hi, mid, lo = _split3(a)
    return _dot(hi, m_bf16) + (_dot(mid, m_bf16) + _dot(lo, m_bf16))


def _dot_exact_lhs(m_bf16, a):
    hi, mid, lo = _split3(a)
    return _dot(m_bf16, hi) + (_dot(m_bf16, mid) + _dot(m_bf16, lo))


def _softplus(x):
    return jnp.maximum(x, 0.0) + jnp.log1p(jnp.exp(-jnp.abs(x)))


def _sigmoid(x):
    return 1.0 / (1.0 + jnp.exp(-x))


def _silu(x):
    return x * _sigmoid(x)


def _tri_masks(reverse):
    ii = lax.broadcasted_iota(jnp.int32, (CHUNK, CHUNK), 0)
    jj = lax.broadcasted_iota(jnp.int32, (CHUNK, CHUNK), 1)
    if reverse:
        return ii <= jj, ii < jj, ii == jj
    return ii >= jj, ii > jj, ii == jj


def _to_col(row, eye):
    return jnp.sum(jnp.where(eye, row, 0.0), axis=1, keepdims=True)


def _ada_kernel(s_ref, w_ref, b_ref, o_ref):
    s = _silu(s_ref[...]).astype(BF16)
    o_ref[...] = _dot(s, w_ref[...].astype(BF16)) + b_ref[...]


def _ada(c2, w_ada, b_ada):
    depth, d, n = w_ada.shape
    tn = 1024
    return pl.pallas_call(
        _ada_kernel,
        grid=(depth, n // tn),
        in_specs=[
            pl.BlockSpec((8, d), lambda l, j: (0, 0)),
            pl.BlockSpec((None, d, tn), lambda l, j: (l, 0, j)),
            pl.BlockSpec((None, 1, tn), lambda l, j: (l, 0, j)),
        ],
        out_specs=pl.BlockSpec((None, 8, tn), lambda l, j: (l, 0, j)),
        out_shape=jax.ShapeDtypeStruct((depth, 8, n), F32),
        compiler_params=_cparams(("arbitrary", "arbitrary")),
        name="ada_mod",
    )(c2, w_ada, b_ada.reshape(depth, 1, n))


def _norm_mod_kernel(x_ref, g_ref, sh_ref, sc_ref, o_ref, *, n_ctx_blk):
    is_ctx = pl.program_id(0) < n_ctx_blk
    x = x_ref[...]
    y = x * lax.rsqrt(jnp.mean(x * x, axis=-1, keepdims=True) + EPS) * g_ref[...]
    sh = jnp.where(is_ctx, sh_ref[1:2, :], sh_ref[0:1, :])
    sc = jnp.where(is_ctx, sc_ref[1:2, :], sc_ref[0:1, :])
    o_ref[...] = (y * (1.0 + sc) + sh).astype(o_ref.dtype)


def _norm_mod(x_all, gain, mod, layer, which_shift, which_scale, n_ctx_blk):
    lt, d = x_all.shape
    return pl.pallas_call(
        functools.partial(_norm_mod_kernel, n_ctx_blk=n_ctx_blk),
        grid=(lt // BLK,),
        in_specs=[
            pl.BlockSpec((BLK, d), lambda i: (i, 0)),
            pl.BlockSpec((1, d), lambda i: (0, 0)),
            pl.BlockSpec((None, 8, d), lambda i: (layer, 0, which_shift)),
            pl.BlockSpec((None, 8, d), lambda i: (layer, 0, which_scale)),
        ],
        out_specs=pl.BlockSpec((BLK, d), lambda i: (i, 0)),
        out_shape=jax.ShapeDtypeStruct((lt, d), BF16),
        compiler_params=_cparams(("arbitrary",)),
        name="norm_mod",
    )(x_all, gain.reshape(1, d), mod, mod)


def _norm_mod_first_kernel(ctx_ref, lat_ref, g_ref, sh_ref, sc_ref, o_ref, xall_ref, *, n_ctx_blk):
    is_ctx = pl.program_id(0) < n_ctx_blk
    x = jnp.where(is_ctx, ctx_ref[...], lat_ref[...])
    xall_ref[...] = x
    y = x * lax.rsqrt(jnp.mean(x * x, axis=-1, keepdims=True) + EPS) * g_ref[...]
    sh = jnp.where(is_ctx, sh_ref[1:2, :], sh_ref[0:1, :])
    sc = jnp.where(is_ctx, sc_ref[1:2, :], sc_ref[0:1, :])
    o_ref[...] = (y * (1.0 + sc) + sh).astype(o_ref.dtype)


def _norm_mod_first(ctx2, lat2, gain, mod, layer, which_shift, which_scale, n_ctx_blk):
    d = lat2.shape[1]
    lt = ctx2.shape[0] + lat2.shape[0]
    return pl.pallas_call(
        functools.partial(_norm_mod_first_kernel, n_ctx_blk=n_ctx_blk),
        grid=(lt // BLK,),
        in_specs=[
            pl.BlockSpec((BLK, d), lambda i: (jnp.minimum(i, n_ctx_blk - 1), 0)),
            pl.BlockSpec((BLK, d), lambda i: (jnp.maximum(i - n_ctx_blk, 0), 0)),
            pl.BlockSpec((1, d), lambda i: (0, 0)),
            pl.BlockSpec((None, 8, d), lambda i: (layer, 0, which_shift)),
            pl.BlockSpec((None, 8, d), lambda i: (layer, 0, which_scale)),
        ],
        out_specs=[pl.BlockSpec((BLK, d), lambda i: (i, 0)), pl.BlockSpec((BLK, d), lambda i: (i, 0))],
        out_shape=[jax.ShapeDtypeStruct((lt, d), BF16), jax.ShapeDtypeStruct((lt, d), F32)],
        compiler_params=_cparams(("arbitrary",)),
        name="norm_mod_first",
    )(ctx2, lat2, gain.reshape(1, d), mod, mod)


def _final_norm_kernel(x_ref, g_ref, o_ref):
    x = x_ref[...]
    o_ref[...] = x * lax.rsqrt(jnp.mean(x * x, axis=-1, keepdims=True) + EPS) * g_ref[...]


def _final_norm(x_all, gain, n_ctx_blk):
    lt, d = x_all.shape
    n_lat = lt - n_ctx_blk * BLK
    return pl.pallas_call(
        _final_norm_kernel,
        grid=(n_lat // BLK,),
        in_specs=[
            pl.BlockSpec((BLK, d), lambda i: (i + n_ctx_blk, 0)),
            pl.BlockSpec((1, d), lambda i: (0, 0)),
        ],
        out_specs=pl.BlockSpec((BLK, d), lambda i: (i, 0)),
        out_shape=jax.ShapeDtypeStruct((n_lat, d), F32),
        compiler_params=_cparams(("arbitrary",)),
        name="final_norm",
    )(x_all, gain.reshape(1, d))


def _mm_epilogue(acc, mode, x_ref, gate_ref, o_ref, row0, n_ctx_rows):
    if mode == "plain":
        o_ref[...] = acc.astype(o_ref.dtype)
    elif mode == "relu2":
        r = jnp.maximum(acc, 0.0)
        o_ref[...] = (r * r).astype(o_ref.dtype)
    else:
        row = row0 + lax.broadcasted_iota(jnp.int32, (acc.shape[0], 1), 0)
        gate = jnp.where(row < n_ctx_rows, gate_ref[1:2, :], gate_ref[0:1, :])
        o_ref[...] = x_ref[...] + gate * acc


def _mm_kernel(*refs, mode, nk, tm, n_ctx_rows):
    if mode == "resid":
        a_ref, b_ref, x_ref, gate_ref = refs[:4]
        rest = refs[4:]
    else:
        a_ref, b_ref = refs[:2]
        x_ref = gate_ref = None
        rest = refs[2:]
    o_ref = rest[0]
    row0 = pl.program_id(1) * tm
    if nk == 1:
        acc = _dot(a_ref[...], b_ref[...])
        _mm_epilogue(acc, mode, x_ref, gate_ref, o_ref, row0, n_ctx_rows)
        return
    acc_ref = rest[1]
    k = pl.program_id(2)

    @pl.when(k == 0)
    def _():
        acc_ref[...] = _dot(a_ref[...], b_ref[...])

    if nk > 2:
        @pl.when(jnp.logical_and(k > 0, k < nk - 1))
        def _():
            acc_ref[...] += _dot(a_ref[...], b_ref[...])

    @pl.when(k == nk - 1)
    def _():
        acc = acc_ref[...] + _dot(a_ref[...], b_ref[...])
        _mm_epilogue(acc, mode, x_ref, gate_ref, o_ref, row0, n_ctx_rows)


def _mm_tiles(m, n, k, mode):
    out_bytes = 4 if mode == "resid" else 2
    for tm in (1280, 1024, 768, 512, 256):
        if m % tm:
            continue
        for tn, tk in ((1024, 4096), (768, 4096), (1024, 2048), (512, 4096), (512, 2048), (512, 1024),
                       (256, 1024)):
            if n % tn or k % min(tk, k):
                continue
            tk = min(tk, k)
            use = 2 * (tm * tk * 2 + tk * tn * 2 + tm * tn * out_bytes) + tm * tn * 4
            if mode == "resid":
                use += 2 * tm * tn * 4
            if k > tk:
                use += tm * tn * 4
            if use <= MM_VMEM_BUDGET:
                return tm, tn, tk
    raise ValueError(f"no matmul tiling for {(m, n, k)}")


def _matmul(a, b, layer, mode="plain", x=None, mod=None, which_gate=0, n_ctx_rows=0):
    m, k = a.shape
    n = b.shape[2]
    tm, tn, tk = _mm_tiles(m, n, k, mode)
    nk = k // tk
    in_specs = [
        pl.BlockSpec((tm, tk), lambda j, i, kk: (i, kk)),
        pl.BlockSpec((None, tk, tn), lambda j, i, kk: (layer, kk, j)),
    ]
    args = [a, b]
    if mode == "resid":
        gate_blk0 = which_gate * (n // tn)
        in_specs += [
            pl.BlockSpec((tm, tn), lambda j, i, kk: (i, j)),
            pl.BlockSpec((None, 8, tn), lambda j, i, kk: (layer, 0, gate_blk0 + j)),
        ]
        args += [x, mod]
    scratch = [pltpu.VMEM((tm, tn), F32)] if nk > 1 else []
    return pl.pallas_call(
        functools.partial(_mm_kernel, mode=mode, nk=nk, tm=tm, n_ctx_rows=n_ctx_rows),
        grid=(n // tn, m // tm, nk),
        in_specs=in_specs,
        out_specs=pl.BlockSpec((tm, tn), lambda j, i, kk: (i, j)),
        out_shape=jax.ShapeDtypeStruct((m, n), F32 if mode == "resid" else BF16),
        scratch_shapes=scratch,
        compiler_params=_cparams(("arbitrary", "arbitrary", "arbitrary")),
        name="mm_" + mode,
    )(*args)


def _gates_t_kernel(w_ref, h_ref, o_ref):
    o_ref[...] = _dot_nt(w_ref[...], h_ref[...])


def _gates_t(ws_t, layer, h):
    lt, d = h.shape
    tm = 1280 if lt % 1280 == 0 else BLK
    return pl.pallas_call(
        _gates_t_kernel,
        grid=(lt // tm,),
        in_specs=[
            pl.BlockSpec((None, LANES, d), lambda i: (layer, 0, 0)),
            pl.BlockSpec((tm, d), lambda i: (i, 0)),
        ],
        out_specs=pl.BlockSpec((LANES, tm), lambda i: (0, i)),
        out_shape=jax.ShapeDtypeStruct((LANES, lt), F32),
        compiler_params=_cparams(("arbitrary",)),
        name="gates_t",
    )(ws_t, h)


def _dn_prep_kernel(prev_ref, main_ref, next_ref, w_ref, o_ref, ext_ref, *, n_ctx_blk, n_blk, n_heads):
    i = pl.program_id(0)
    prev_ok = jnp.logical_and(i != 0, i != n_ctx_blk)
    next_ok = jnp.logical_and(i != n_ctx_blk - 1, i != n_blk - 1)
    ext_ref[0:CONV_HALO, :] = jnp.where(prev_ok, prev_ref[...].astype(F32), 0.0)
    ext_ref[CONV_HALO:CONV_HALO + BLK, :] = main_ref[...].astype(F32)
    ext_ref[CONV_HALO + BLK:, :] = jnp.where(next_ok, next_ref[...].astype(F32), 0.0)
    for g in range(3 * n_heads):
        cs = slice(g * HEAD_DIM, (g + 1) * HEAD_DIM)
        acc = None
        for tap in range(CONV_W):
            win = ext_ref[pl.ds(CONV_HALO - CONV_W // 2 + tap, BLK), cs]
            term = win * w_ref[tap:tap + 1, cs]
            acc = term if acc is None else acc + term
        y = _silu(acc)
        if g < 2 * n_heads:
            y = y * lax.rsqrt(jnp.sum(y * y, axis=-1, keepdims=True) + EPS)
            if g < n_heads:
                y = y * HEAD_DIM ** -0.5
        o_ref[:, cs] = y.astype(o_ref.dtype)


def _dn_prep(proj, conv_w, n_heads, n_ctx_blk):
    lt = proj.shape[0]
    c = 3 * n_heads * HEAD_DIM
    n_blk = lt // BLK
    per = BLK // CONV_HALO
    n_halo = lt // CONV_HALO
    return pl.pallas_call(
        functools.partial(_dn_prep_kernel, n_ctx_blk=n_ctx_blk, n_blk=n_blk, n_heads=n_heads),
        grid=(n_blk,),
        in_specs=[
            pl.BlockSpec((CONV_HALO, c), lambda i: (jnp.maximum(i * per - 1, 0), 0)),
            pl.BlockSpec((BLK, c), lambda i: (i, 0)),
            pl.BlockSpec((CONV_HALO, c), lambda i: (jnp.minimum((i + 1) * per, n_halo - 1), 0)),
            pl.BlockSpec((CONV_W, c), lambda i: (0, 0)),
        ],
        out_specs=pl.BlockSpec((BLK, c), lambda i: (i, 0)),
        out_shape=jax.ShapeDtypeStruct((lt, c), BF16),
        scratch_shapes=[pltpu.VMEM((BLK + 2 * CONV_HALO, c), F32)],
        compiler_params=_cparams(("arbitrary",)),
        name="dn_prep",
    )(proj, proj, proj, conv_w)


def _fwd_blk(s):
    return s


def _bwd_blk(s, n_ctx_blk, n_blk):
    return jnp.where(s < n_ctx_blk, n_ctx_blk - 1 - s, n_blk - 1 - (s - n_ctx_blk))


INV_BASE = 8
DN_HEADS_PER_STEP = 4
RET_HEADS_PER_STEP = 5


def _unit_triangular_inverses(a_list):
    ii = lax.broadcasted_iota(jnp.int32, (CHUNK, CHUNK), 0)
    jj = lax.broadcasted_iota(jnp.int32, (CHUNK, CHUNK), 1)

    def mul(x, y):
        return _dot(x.astype(BF16), y.astype(BF16))

    diag_blk = ii // INV_BASE == jj // INV_BASE
    eye_f = jnp.where(ii == jj, 1.0, 0.0)
    d = [jnp.where(diag_blk, a, 0.0) for a in a_list]
    m = [eye_f - x for x in d]
    p = [mul(x, x) for x in d]
    size = 2
    while size < INV_BASE:
        m = [mi + mul(mi, pi) for mi, pi in zip(m, p)]
        size *= 2
        if size < INV_BASE:
            p = [mul(pi, pi) for pi in p]
    size = INV_BASE
    while size < CHUNK:
        pair = jnp.logical_and(ii // (2 * size) == jj // (2 * size), ii // size != jj // size)
        t = [mul(mi, jnp.where(pair, a, 0.0)) for mi, a in zip(m, a_list)]
        m = [mi - mul(ti, mi) for mi, ti in zip(m, t)]
        size *= 2
    return m


def _alternate(per_dir):
    return [it for pair in zip(*per_dir) for it in pair]


def _chunk_order(reverse):
    n_chunks = BLK // CHUNK
    return range(n_chunks - 1, -1, -1) if reverse else range(n_chunks)


def _dn_scan_kernel(alog_ref, dtb_ref, qf, kf, vf, daf, dbf, qb, kb, vb, dab, dbb, of_ref, ob_ref, s_ref,
                    *, hg):
    @pl.when(pl.program_id(1) == 0)
    def _():
        s_ref[...] = jnp.zeros_like(s_ref)

    ii = lax.broadcasted_iota(jnp.int32, (CHUNK, CHUNK), 0)
    jj = lax.broadcasted_iota(jnp.int32, (CHUNK, CHUNK), 1)
    o_refs = (of_ref, ob_ref)
    chains = []
    for d, (q_ref, k_ref, v_ref, da_ref, db_ref) in enumerate(((qf, kf, vf, daf, dbf), (qb, kb, vb, dab, dbb))):
        reverse = d == 1
        incl, strict, eye = _tri_masks(reverse)
        cum = jnp.where((ii >= jj) if reverse else (ii <= jj), 1.0, 0.0).astype(BF16)
        last = 0 if reverse else CHUNK - 1
        for j in range(hg):
            h = pl.program_id(0) * hg + j
            cols = slice(j * HEAD_DIM, (j + 1) * HEAD_DIM)
            neg_a = -jnp.exp(jnp.full((1, CHUNK), alog_ref[d, h], F32))
            g = neg_a * _softplus(da_ref[j] + dtb_ref[d, h])
            beta = _sigmoid(db_ref[j])
            gc_rows = _dot_exact_rhs(g, cum)
            items = []
            for c in _chunk_order(reverse):
                rows = slice(c * CHUNK, (c + 1) * CHUNK)
                gc_r = gc_rows[c:c + 1, :]
                items.append(dict(
                    d=d, chain=d * hg + j, rows=rows, cols=cols, incl=incl, strict=strict, gc_r=gc_r,
                    gc_c=_to_col(gc_r, eye), b_c=_to_col(beta[c:c + 1, :], eye),
                    g_last=gc_r[:, last:last + 1],
                    q=q_ref[rows, cols], k=k_ref[rows, cols], v=v_ref[rows, cols]))
            chains.append(items)
    items = _alternate(chains)

    for it in items:
        it["kk"] = _dot_nt(it["k"], it["k"])
        it["qk"] = _dot_nt(it["q"], it["k"])
    for it in items:
        decay = jnp.exp(jnp.where(it["incl"], it["gc_c"] - it["gc_r"], -jnp.inf))
        it["a"] = jnp.where(it["strict"], it["kk"] * decay, 0.0) * it["b_c"]
        it["qk"] = (it["qk"] * decay).astype(BF16)
        e_gc = jnp.exp(it["gc_c"])
        kf32 = it["k"].astype(F32)
        it["rhs"] = jnp.concatenate(
            [(kf32 * (it["b_c"] * e_gc)).astype(BF16), (it["v"].astype(F32) * it["b_c"]).astype(BF16)], axis=1)
        it["q_dec"] = it["q"].astype(F32) * e_gc
        it["k_dec"] = (kf32 * jnp.exp(it["g_last"] - it["gc_c"])).astype(BF16)
    t_inv = _unit_triangular_inverses([it["a"] for it in items])
    for it, t in zip(items, t_inv):
        it["wu"] = _dot(t.astype(BF16), it["rhs"]).astype(BF16)
    for it in items:
        it["qkwu"] = _dot(it["qk"], it["wu"])
        it["kdwu"] = _dot_tn(it["k_dec"], it["wu"])
    for it in items:
        it["q2"] = (it["q_dec"] - it["qkwu"][:, :HEAD_DIM]).astype(BF16)
        it["m1"] = it["kdwu"][:, :HEAD_DIM].astype(BF16)

    s = [s_ref[n] for n in range(2 * hg)]
    for it in items:
        n = it["chain"]
        s_b = s[n].astype(BF16)
        o = _dot(it["q2"], s_b) + it["qkwu"][:, HEAD_DIM:]
        o_ref = o_refs[it["d"]]
        o_ref[it["rows"], it["cols"]] = o.astype(o_ref.dtype)
        s[n] = jnp.exp(it["g_last"]) * s[n] - _dot(it["m1"], s_b) + it["kdwu"][:, HEAD_DIM:]
    for n in range(2 * hg):
        s_ref[n] = s[n]


def _dn_scan(dnp, gates4, a_log, dt_bias, n_heads, n_ctx_blk):
    lt = dnp.shape[0]
    n_blk = lt // BLK
    bwd = functools.partial(_bwd_blk, n_ctx_blk=n_ctx_blk, n_blk=n_blk)
    n_chunks = BLK // CHUNK
    hg = DN_HEADS_PER_STEP
    assert n_heads % hg == 0
    n_grp = n_heads // hg
    width = hg * HEAD_DIM

    def col(grp_off, blk_fn):
        return pl.BlockSpec((BLK, width), lambda g, s: (blk_fn(s), grp_off + g))

    def gate(grp_off, blk_fn):
        return pl.BlockSpec((hg, None, n_chunks, CHUNK), lambda g, s: (grp_off + g, blk_fn(s), 0, 0))

    smem = pl.BlockSpec(memory_space=pltpu.SMEM)
    out_sds = jax.ShapeDtypeStruct((lt, n_heads * HEAD_DIM), BF16)
    return pl.pallas_call(
        functools.partial(_dn_scan_kernel, hg=hg),
        grid=(n_grp, n_blk),
        in_specs=[
            smem, smem,
            col(0, _fwd_blk), col(n_grp, _fwd_blk), col(2 * n_grp, _fwd_blk),
            gate(0, _fwd_blk), gate(2 * n_grp, _fwd_blk),
            col(0, bwd), col(n_grp, bwd), col(2 * n_grp, bwd),
            gate(n_grp, bwd), gate(3 * n_grp, bwd),
        ],
        out_specs=[
            pl.BlockSpec((BLK, width), lambda g, s: (s, g)),
            pl.BlockSpec((BLK, width), lambda g, s: (bwd(s), g)),
        ],
        out_shape=[out_sds, out_sds],
        scratch_shapes=[pltpu.VMEM((2 * hg, HEAD_DIM, HEAD_DIM), F32)],
        compiler_params=_cparams(("arbitrary", "arbitrary")),
        name="dn_scan",
    )(a_log, dt_bias, dnp, dnp, dnp, gates4, gates4, dnp, dnp, dnp, gates4, gates4)


def _gla_scan_kernel(qf, kf, vf, smf, gwf, gbf, qb, kb, vb, smb, gwb, gbb, of_ref, ob_ref, st_ref, *, dk, pg):
    @pl.when(pl.program_id(1) == 0)
    def _():
        st_ref[...] = jnp.zeros_like(st_ref)

    ii = lax.broadcasted_iota(jnp.int32, (CHUNK, CHUNK), 0)
    jj = lax.broadcasted_iota(jnp.int32, (CHUNK, CHUNK), 1)
    lane = lax.broadcasted_iota(jnp.int32, (1, LANES), 1)
    n_sub = LANES // dk
    sels = [(lane // dk) == sub for sub in range(n_sub)]
    o_refs = (of_ref, ob_ref)
    chains = []
    for d, (q_ref, k_ref, v_ref, sm_ref, gw_ref, gb_ref) in enumerate(
            ((qf, kf, vf, smf, gwf, gbf), (qb, kb, vb, smb, gwb, gbb))):
        reverse = d == 1
        incl, _, _ = _tri_masks(reverse)
        cum = jnp.where((ii <= jj) if reverse else (ii >= jj), 1.0, 0.0).astype(BF16)
        z = _dot(sm_ref[...], gw_ref[...]) + gb_ref[...]
        log_a = (jnp.minimum(z, 0.0) - jnp.log1p(jnp.exp(-jnp.abs(z)))) * (1.0 / GLA_GATE_NORM)
        last = 0 if reverse else CHUNK - 1
        for j in range(pg):
            cols = slice(j * LANES, (j + 1) * LANES)
            vcols = [slice((j * n_sub + sub) * HEAD_DIM, (j * n_sub + sub + 1) * HEAD_DIM) for sub in range(n_sub)]
            items = []
            for c in _chunk_order(reverse):
                rows = slice(c * CHUNK, (c + 1) * CHUNK)
                items.append(dict(d=d, chain=d * pg + j, rows=rows, vcols=vcols, incl=incl, cum=cum, last=last,
                                  log_a=log_a[rows, cols], q=q_ref[rows, cols], k=k_ref[rows, cols],
                                  v=[v_ref[rows, vc] for vc in vcols]))
            chains.append(items)
    items = _alternate(chains)

    for it in items:
        it["gc"] = _dot_exact_lhs(it["cum"], it["log_a"])
    for it in items:
        gc = it["gc"]
        g_last = gc[it["last"]:it["last"] + 1, :]
        kf32 = it["k"].astype(F32)
        q_in = it["q"].astype(F32) * jnp.exp(gc) * dk ** -0.5
        it["qm"] = [jnp.where(sel, q_in, 0.0).astype(BF16) for sel in sels]
        it["k_in"] = (kf32 * jnp.exp(-gc)).astype(BF16)
        it["k_dec"] = (kf32 * jnp.exp(g_last - gc)).astype(BF16)
        it["decay"] = jnp.exp(g_last)
    for it in items:
        it["scores"] = [jnp.where(it["incl"], _dot_nt(qm, it["k_in"]), 0.0).astype(BF16) for qm in it["qm"]]
        upd = None
        for sel, v in zip(sels, it["v"]):
            u = _dot_tn(v, it["k_dec"])
            upd = u if upd is None else jnp.where(sel, u, upd)
        it["upd"] = upd
    for it in items:
        it["intra"] = [_dot(sc, v) for sc, v in zip(it["scores"], it["v"])]

    st = [st_ref[n] for n in range(2 * pg)]
    for it in items:
        n = it["chain"]
        st_b = st[n].astype(BF16)
        o_ref = o_refs[it["d"]]
        for sub in range(n_sub):
            o = it["intra"][sub] + _dot_nt(it["qm"][sub], st_b)
            o_ref[it["rows"], it["vcols"][sub]] = o.astype(o_ref.dtype)
        st[n] = st[n] * it["decay"] + it["upd"]
    for n in range(2 * pg):
        st_ref[n] = st[n]


def _gla_scan(proj, gw_full, gate_b, layer, lay, n_ctx_blk):
    lt = proj.shape[0]
    n_blk = lt // BLK
    n_heads, dk = lay["h_gla"], lay["gla_dk"]
    n_sub = LANES // dk
    pg = n_heads // n_sub
    qw, vw = pg * LANES, pg * n_sub * HEAD_DIM
    assert lay["gq"] % qw == 0 and lay["gk"] % qw == 0 and lay["gv"] % vw == 0
    bwd = functools.partial(_bwd_blk, n_ctx_blk=n_ctx_blk, n_blk=n_blk)
    q0, k0, v0, sm0 = lay["gq"] // qw, lay["gk"] // qw, lay["gv"] // vw, lay["small"] // LANES

    def specs(d, blk_fn):
        return [
            pl.BlockSpec((BLK, qw), lambda p, s: (blk_fn(s), q0 + p)),
            pl.BlockSpec((BLK, qw), lambda p, s: (blk_fn(s), k0 + p)),
            pl.BlockSpec((BLK, vw), lambda p, s: (blk_fn(s), v0 + p)),
            pl.BlockSpec((BLK, LANES), lambda p, s: (blk_fn(s), sm0)),
            pl.BlockSpec((None, None, LANES, qw), lambda p, s: (layer, d, 0, p)),
            pl.BlockSpec((None, None, 1, qw), lambda p, s: (layer, d, 0, p)),
        ]

    out_sds = jax.ShapeDtypeStruct((lt, n_heads * HEAD_DIM), BF16)
    return pl.pallas_call(
        functools.partial(_gla_scan_kernel, dk=dk, pg=pg),
        grid=(n_heads // (n_sub * pg), n_blk),
        in_specs=specs(0, _fwd_blk) + specs(1, bwd),
        out_specs=[
            pl.BlockSpec((BLK, vw), lambda p, s: (s, p)),
            pl.BlockSpec((BLK, vw), lambda p, s: (bwd(s), p)),
        ],
        out_shape=[out_sds, out_sds],
        scratch_shapes=[pltpu.VMEM((2 * pg, HEAD_DIM, LANES), F32)],
        compiler_params=_cparams(("arbitrary", "arbitrary")),
        name="gla_scan",
    )(proj, proj, proj, proj, gw_full, gate_b, proj, proj, proj, proj, gw_full, gate_b)


def _rope(t, cos, sin_signed, first_half):
    partner = jnp.where(first_half, pltpu.roll(t, LANES - HEAD_DIM // 4, 1), pltpu.roll(t, HEAD_DIM // 4, 1))
    return t * cos + partner * sin_signed


def _ret_scan_kernel(lg_ref, qf, kf, vf, cf, sf, qb, kb, vb, cb, sb, of_ref, ob_ref, s_ref, *, hg):
    @pl.when(pl.program_id(1) == 0)
    def _():
        s_ref[...] = jnp.zeros_like(s_ref)

    ii = lax.broadcasted_iota(jnp.int32, (CHUNK, CHUNK), 0)
    jj = lax.broadcasted_iota(jnp.int32, (CHUNK, CHUNK), 1)
    pos = lax.broadcasted_iota(jnp.int32, (CHUNK, 1), 0).astype(F32)
    lane = lax.broadcasted_iota(jnp.int32, (1, LANES), 1)
    first_half = (lane % (HEAD_DIM // 2)) < (HEAD_DIM // 4)
    o_refs = (of_ref, ob_ref)
    lgs = [lg_ref[pl.program_id(0) * hg + j] for j in range(hg)]
    g_chunk = [jnp.exp(jnp.full((1, LANES), lg * CHUNK, F32)) for lg in lgs]
    chains = []
    for d, (q_ref, k_ref, v_ref, cos_ref, sin_ref) in enumerate(((qf, kf, vf, cf, sf), (qb, kb, vb, cb, sb))):
        reverse = d == 1
        incl, _, _ = _tri_masks(reverse)
        dist = ((jj - ii) if reverse else (ii - jj)).astype(F32)
        tables = [(cos_ref[slice(c * CHUNK, (c + 1) * CHUNK), :], sin_ref[slice(c * CHUNK, (c + 1) * CHUNK), :])
                  for c in range(BLK // CHUNK)]
        for j, lg in enumerate(lgs):
            cols = slice(j * HEAD_DIM, (j + 1) * HEAD_DIM)
            decay = jnp.exp(jnp.where(incl, lg * dist, -jnp.inf))
            if reverse:
                q_scale = jnp.exp(lg * (CHUNK - pos))
                k_scale = jnp.exp(lg * pos)
            else:
                q_scale = jnp.exp(lg * (pos + 1.0))
                k_scale = jnp.exp(lg * (CHUNK - 1.0 - pos))
            items = []
            for c in _chunk_order(reverse):
                rows = slice(c * CHUNK, (c + 1) * CHUNK)
                cos, sin = tables[c]
                q = _rope(q_ref[rows, cols].astype(F32) * HEAD_DIM ** -0.5, cos, sin, first_half)
                k = _rope(k_ref[rows, cols].astype(F32), cos, sin, first_half)
                items.append(dict(d=d, chain=d * hg + j, head=j, rows=rows, cols=cols, decay=decay,
                                  q=q.astype(BF16), k=k.astype(BF16),
                                  q_dec=(q * q_scale).astype(BF16), k_dec=(k * k_scale).astype(BF16),
                                  v=v_ref[rows, cols]))
            chains.append(items)
    items = _alternate(chains)

    for it in items:
        it["scores"] = (_dot_nt(it["q"], it["k"]) * it["decay"]).astype(BF16)
        it["upd"] = _dot_tn(it["k_dec"], it["v"])
    for it in items:
        it["intra"] = _dot(it["scores"], it["v"])

    s = [s_ref[n] for n in range(2 * hg)]
    for it in items:
        n = it["chain"]
        o = it["intra"] + _dot(it["q_dec"], s[n].astype(BF16))
        o_ref = o_refs[it["d"]]
        o_ref[it["rows"], it["cols"]] = o.astype(o_ref.dtype)
        s[n] = g_chunk[it["head"]] * s[n] + it["upd"]
    for n in range(2 * hg):
        s_ref[n] = s[n]


def _ret_scan(proj, cos_t, sin_t, log_gamma, lay, n_ctx_blk):
    lt = proj.shape[0]
    n_blk = lt // BLK
    n_heads = lay["h_ret"]
    hg = RET_HEADS_PER_STEP
    width = hg * HEAD_DIM
    assert n_heads % hg == 0 and lay["rq"] % width == 0 and lay["rk"] % width == 0 and lay["rv"] % width == 0
    bwd = functools.partial(_bwd_blk, n_ctx_blk=n_ctx_blk, n_blk=n_blk)
    q0, k0, v0 = lay["rq"] // width, lay["rk"] // width, lay["rv"] // width

    def specs(blk_fn):
        return [
            pl.BlockSpec((BLK, width), lambda g, s: (blk_fn(s), q0 + g)),
            pl.BlockSpec((BLK, width), lambda g, s: (blk_fn(s), k0 + g)),
            pl.BlockSpec((BLK, width), lambda g, s: (blk_fn(s), v0 + g)),
            pl.BlockSpec((BLK, HEAD_DIM), lambda g, s: (blk_fn(s), 0)),
            pl.BlockSpec((BLK, HEAD_DIM), lambda g, s: (blk_fn(s), 0)),
        ]

    out_sds = jax.ShapeDtypeStruct((lt, n_heads * HEAD_DIM), BF16)
    return pl.pallas_call(
        functools.partial(_ret_scan_kernel, hg=hg),
        grid=(n_heads // hg, n_blk),
        in_specs=[pl.BlockSpec(memory_space=pltpu.SMEM)] + specs(_fwd_blk) + specs(bwd),
        out_specs=[
            pl.BlockSpec((BLK, width), lambda g, s: (s, g)),
            pl.BlockSpec((BLK, width), lambda g, s: (bwd(s), g)),
        ],
        out_shape=[out_sds, out_sds],
        scratch_shapes=[pltpu.VMEM((2 * hg, HEAD_DIM, HEAD_DIM), F32)],
        compiler_params=_cparams(("arbitrary", "arbitrary")),
        name="ret_scan",
    )(log_gamma, proj, proj, proj, cos_t, sin_t, proj, proj, proj, cos_t, sin_t)


def _finish_kernel(dnf, dnb, dz, glf, glb, gg, rtf, rtb, rg, g_dn, g_gla, g_ret, o_ref, *, h_dn, h_gla, h_ret):
    col = 0
    for (of, ob, gate, gain, n_heads, centred) in (
        (dnf, dnb, dz, g_dn, h_dn, False),
        (glf, glb, gg, g_gla, h_gla, False),
        (rtf, rtb, rg, g_ret, h_ret, True),
    ):
        for h in range(n_heads):
            cs = slice(h * HEAD_DIM, (h + 1) * HEAD_DIM)
            o = of[:, cs].astype(F32) + ob[:, cs].astype(F32)
            if centred:
                o = o - jnp.mean(o, axis=-1, keepdims=True)
            y = o * lax.rsqrt(jnp.mean(o * o, axis=-1, keepdims=True) + EPS) * gain[...]
            y = y * _silu(gate[:, cs].astype(F32))
            o_ref[:, col:col + HEAD_DIM] = y.astype(o_ref.dtype)
            col += HEAD_DIM


def _finish(proj, dn_o, gla_o, ret_o, g_dn, g_gla, g_ret, lay):
    lt = proj.shape[0]
    h_dn, h_gla, h_ret = lay["h_dn"], lay["h_gla"], lay["h_ret"]
    w_dn, w_gla, w_ret = h_dn * HEAD_DIM, h_gla * HEAD_DIM, h_ret * HEAD_DIM
    d_mix = w_dn + w_gla + w_ret

    def rows(w, cblk=0):
        return pl.BlockSpec((BLK, w), lambda i: (i, cblk))

    gain = pl.BlockSpec((1, HEAD_DIM), lambda i: (0, 0))
    return pl.pallas_call(
        functools.partial(_finish_kernel, h_dn=h_dn, h_gla=h_gla, h_ret=h_ret),
        grid=(lt // BLK,),
        in_specs=[
            rows(w_dn), rows(w_dn), rows(w_dn, lay["dz"] // w_dn),
            rows(w_gla), rows(w_gla), rows(w_gla, lay["gg"] // w_gla),
            rows(w_ret), rows(w_ret), rows(w_ret, lay["rg"] // w_ret),
            gain, gain, gain,
        ],
        out_specs=rows(d_mix),
        out_shape=jax.ShapeDtypeStruct((lt, d_mix), BF16),
        compiler_params=_cparams(("arbitrary",)),
        name="finish",
    )(dn_o[0], dn_o[1], proj, gla_o[0], gla_o[1], proj, ret_o[0], ret_o[1], proj,
      g_dn.reshape(1, HEAD_DIM), g_gla.reshape(1, HEAD_DIM), g_ret.reshape(1, HEAD_DIM))


def _layout(d_model):
    n_heads = d_model // HEAD_DIM
    h_dn = (3 * d_model) // (8 * HEAD_DIM)
    h_gla = (5 * d_model) // (16 * HEAD_DIM)
    h_ret = n_heads - h_dn - h_gla
    gla_dk = HEAD_DIM // 2
    d_dn, d_gk, d_gv, d_ret = h_dn * HEAD_DIM, h_gla * gla_dk, h_gla * HEAD_DIM, h_ret * HEAD_DIM
    sizes = dict(dq=d_dn, dk=d_dn, dv=d_dn, dz=d_dn, da_f=h_dn, da_b=h_dn, db_f=h_dn, db_b=h_dn,
                 gq=d_gk, gk=d_gk, gv=d_gv, gg=d_gv, glr_f=GLA_GATE_RANK, glr_b=GLA_GATE_RANK,
                 rq=d_ret, rk=d_ret, rv=d_ret, rg=d_ret)
    src, off = {}, 0
    for name, size in sizes.items():
        src[name] = (off, size)
        off += size
    assert d_gv == d_ret == 2 * d_gk and 4 * h_dn + 2 * GLA_GATE_RANK <= LANES
    lay = dict(h_dn=h_dn, h_gla=h_gla, h_ret=h_ret, gla_dk=gla_dk, src=src, d_in=off)
    pos = 0
    order = []
    for name in ("dq", "dk", "dv", "dz"):
        lay[name] = pos
        order.append((name, pos))
        pos += d_dn
    lay["small"] = pos
    small_names = ("da_f", "da_b", "db_f", "db_b", "glr_f", "glr_b")
    sp = pos
    for name in small_names:
        lay[name] = sp
        order.append((name, sp))
        sp += sizes[name]
    pos += LANES
    pos = -(-pos // d_gv) * d_gv
    for name in ("gq", "gk"):
        lay[name] = pos
        order.append((name, pos))
        pos += d_gk
    for name in ("gv", "gg", "rq", "rk", "rv", "rg"):
        lay[name] = pos
        order.append((name, pos))
        pos += d_gv
    lay["n_cols"] = pos
    lay["order"] = order
    lay["glr_lane"] = (lay["glr_f"] - lay["small"], lay["glr_b"] - lay["small"])
    return lay


def _relayout_w_in(w_in, lay):
    depth, d, _ = w_in.shape
    out = jnp.zeros((depth, d, lay["n_cols"]), BF16)
    for name, pos in lay["order"]:
        s0, size = lay["src"][name]
        out = lax.dynamic_update_slice(out, w_in[:, :, s0:s0 + size].astype(BF16), (0, 0, pos))
    return out


def _rope_tables(n_ctx, n_lat):
    quarter = HEAD_DIM // 4
    inv = ROPE_BASE ** (-jnp.arange(quarter, dtype=F32) / quarter)
    t = jnp.arange(n_lat, dtype=jnp.int32)
    row = (t // GRID_W).astype(F32)[:, None] * inv
    colp = (t % GRID_W).astype(F32)[:, None] * inv
    cos = jnp.concatenate([jnp.cos(row), jnp.cos(row), jnp.cos(colp), jnp.cos(colp)], axis=-1)
    sin = jnp.concatenate([-jnp.sin(row), jnp.sin(row), -jnp.sin(colp), jnp.sin(colp)], axis=-1)
    cos = jnp.concatenate([jnp.ones((n_ctx, HEAD_DIM), F32), cos], axis=0)
    sin = jnp.concatenate([jnp.zeros((n_ctx, HEAD_DIM), F32), sin], axis=0)
    return cos, sin


def kernel(x, c, ctx, c_ctx, w_ada, b_ada, norm1_g, w_in, dn_conv, dn_a_log, dn_dt_bias, dn_norm_g,
           gla_gate_w, gla_gate_b, gla_norm_g, ret_norm_g, w_out, norm2_g, w_ff1, w_ff2, final_norm_g):
    batch, n_lat, d = x.shape
    n_ctx = ctx.shape[1]
    depth = w_in.shape[0]
    assert batch == 1 and n_ctx % BLK == 0 and n_lat % BLK == 0 and n_lat % GRID_W == 0
    lay = _layout(d)
    assert lay["d_in"] == w_in.shape[2]
    n_ctx_blk = n_ctx // BLK
    lt = n_ctx + n_lat
    h_dn, h_gla, h_ret = lay["h_dn"], lay["h_gla"], lay["h_ret"]

    w_in_b = _relayout_w_in(w_in, lay)
    s0 = lay["src"]["da_f"][0]
    ws_t = jnp.zeros((depth, LANES, d), BF16)
    ws_t = lax.dynamic_update_slice(
        ws_t, jnp.swapaxes(w_in[:, :, s0:s0 + 4 * h_dn], 1, 2).astype(BF16), (0, 0, 0))
    w_out_b = w_out.astype(BF16)
    w_ff1_b = w_ff1.astype(BF16)
    w_ff2_b = w_ff2.astype(BF16)
    gw_full = jnp.zeros((depth, 2, LANES, gla_gate_w.shape[-1]), BF16)
    for dd in range(2):
        gw_full = lax.dynamic_update_slice(
            gw_full, gla_gate_w[:, dd:dd + 1].astype(BF16), (0, dd, lay["glr_lane"][dd], 0))
    gate_b4 = gla_gate_b.reshape(depth, 2, 1, -1)
    cos_t, sin_t = _rope_tables(n_ctx, n_lat)
    log_gamma = jnp.log1p(-jnp.exp2(-5.0 - jnp.arange(h_ret, dtype=F32)))

    c2 = jnp.zeros((8, d), F32).at[0].set(c[0]).at[1].set(c_ctx)
    mod = _ada(c2, w_ada, b_ada)
    x_all = None

    for i in range(depth):
        if i == 0:
            h, x_all = _norm_mod_first(ctx[0], x[0], norm1_g[0], mod, 0, 0, 1, n_ctx_blk)
        else:
            h = _norm_mod(x_all, norm1_g[i], mod, i, 0, 1, n_ctx_blk)
        proj = _matmul(h, w_in_b, i)
        gates4 = _gates_t(ws_t, i, h).reshape(LANES, lt // BLK, BLK // CHUNK, CHUNK)
        dnp = _dn_prep(proj, dn_conv[i], h_dn, n_ctx_blk)
        dn_o = _dn_scan(dnp, gates4, dn_a_log[i], dn_dt_bias[i], h_dn, n_ctx_blk)
        gla_o = _gla_scan(proj, gw_full, gate_b4, i, lay, n_ctx_blk)
        ret_o = _ret_scan(proj, cos_t, sin_t, log_gamma, lay, n_ctx_blk)
        y = _finish(proj, dn_o, gla_o, ret_o, dn_norm_g[i], gla_norm_g[i], ret_norm_g[i], lay)
        x_all = _matmul(y, w_out_b, i, "resid", x_all, mod, 2, n_ctx)
        h2 = _norm_mod(x_all, norm2_g[i], mod, i, 3, 4, n_ctx_blk)
        hid = _matmul(h2, w_ff1_b, i, "relu2")
        x_all = _matmul(hid, w_ff2_b, i, "resid", x_all, mod, 5, n_ctx)
    return _final_norm(x_all, final_norm_g, n_ctx_blk)[None]
```

```python
import functools
import math

import numpy as np
import jax
import jax.numpy as jnp
from jax import lax
from jax.experimental import pallas as pl
from jax.experimental.pallas import tpu as pltpu

F32 = jnp.float32
BF16 = jnp.bfloat16

HEAD_DIM = 128
CHUNK = 64
GRID_W = 64
CONV_W = 5
CONV_HALO = 16
GLA_GATE_RANK = 16
GLA_GATE_NORM = 16.0
EPS = 1e-6
ROPE_BASE = 10000.0
LANES = 128
BLK = 4 * CHUNK
VMEM_LIMIT = 60000 * 1024
MM_VMEM_BUDGET = 52 * 1024 * 1024


def _cparams(sem):
    return pltpu.CompilerParams(dimension_semantics=sem, vmem_limit_bytes=VMEM_LIMIT)


def _dot(a, b):
    return jnp.dot(a, b, preferred_element_type=F32)


def _dot_nt(a, b):
    return lax.dot_general(a, b, (((1,), (1,)), ((), ())), preferred_element_type=F32)


def _dot_tn(a, b):
    return lax.dot_general(a, b, (((0,), (0,)), ((), ())), preferred_element_type=F32)


def _split3(a):
    hi = a.astype(BF16)
    r = a - hi.astype(F32)
    mid = r.astype(BF16)
    lo = (r - mid.astype(F32)).astype(BF16)
    return hi, mid, lo


def _dot_exact_rhs(a, m_bf16):
    hi, mid, lo = _split3(a)
    return _dot(hi, m_bf16) + (_dot(mid, m_bf16) + _dot(lo, m_bf16))


def _dot_exact_lhs(m_bf16, a):
    hi, mid, lo = _split3(a)
    return _dot(m_bf16, hi) + (_dot(m_bf16, mid) + _dot(m_bf16, lo))


def _softplus(x):
    return jnp.maximum(x, 0.0) + jnp.log1p(jnp.exp(-jnp.abs(x)))


def _sigmoid(x):
    return 1.0 / (1.0 + jnp.exp(-x))


def _silu(x):
    return x * _sigmoid(x)


def _tri_masks(reverse):
    ii = lax.broadcasted_iota(jnp.int32, (CHUNK, CHUNK), 0)
    jj = lax.broadcasted_iota(jnp.int32, (CHUNK, CHUNK), 1)
    if reverse:
        return ii <= jj, ii < jj, ii == jj
    return ii >= jj, ii > jj, ii == jj


def _to_col(row, eye):
    return jnp.sum(jnp.where(eye, row, 0.0), axis=1, keepdims=True)


def _ada_kernel(s_ref, w_ref, b_ref, o_ref):
    s = _silu(s_ref[...]).astype(BF16)
    o_ref[...] = _dot(s, w_ref[...].astype(BF16)) + b_ref[...]


def _ada(c2, w_ada, b_ada):
    depth, d, n = w_ada.shape
    tn = 1024
    return pl.pallas_call(
        _ada_kernel,
        grid=(depth, n // tn),
        in_specs=[
            pl.BlockSpec((8, d), lambda l, j: (0, 0)),
            pl.BlockSpec((None, d, tn), lambda l, j: (l, 0, j)),
            pl.BlockSpec((None, 1, tn), lambda l, j: (l, 0, j)),
        ],
        out_specs=pl.BlockSpec((None, 8, tn), lambda l, j: (l, 0, j)),
        out_shape=jax.ShapeDtypeStruct((depth, 8, n), F32),
        compiler_params=_cparams(("arbitrary", "arbitrary")),
        name="ada_mod",
    )(c2, w_ada, b_ada.reshape(depth, 1, n))


def _norm_mod_kernel(x_ref, g_ref, sh_ref, sc_ref, o_ref, *, n_ctx_blk):
    is_ctx = pl.program_id(0) < n_ctx_blk
    x = x_ref[...]
    y = x * lax.rsqrt(jnp.mean(x * x, axis=-1, keepdims=True) + EPS) * g_ref[...]
    sh = jnp.where(is_ctx, sh_ref[1:2, :], sh_ref[0:1, :])
    sc = jnp.where(is_ctx, sc_ref[1:2, :], sc_ref[0:1, :])
    o_ref[...] = (y * (1.0 + sc) + sh).astype(o_ref.dtype)


def _norm_mod(x_all, gain, mod, layer, which_shift, which_scale, n_ctx_blk):
    lt, d = x_all.shape
    return pl.pallas_call(
        functools.partial(_norm_mod_kernel, n_ctx_blk=n_ctx_blk),
        grid=(lt // BLK,),
        in_specs=[
            pl.BlockSpec((BLK, d), lambda i: (i, 0)),
            pl.BlockSpec((1, d), lambda i: (0, 0)),
            pl.BlockSpec((None, 8, d), lambda i: (layer, 0, which_shift)),
            pl.BlockSpec((None, 8, d), lambda i: (layer, 0, which_scale)),
        ],
        out_specs=pl.BlockSpec((BLK, d), lambda i: (i, 0)),
        out_shape=jax.ShapeDtypeStruct((lt, d), BF16),
        compiler_params=_cparams(("arbitrary",)),
        name="norm_mod",
    )(x_all, gain.reshape(1, d), mod, mod)


def _norm_mod_first_kernel(ctx_ref, lat_ref, g_ref, sh_ref, sc_ref, o_ref, xall_ref, *, n_ctx_blk):
    is_ctx = pl.program_id(0) < n_ctx_blk
    x = jnp.where(is_ctx, ctx_ref[...], lat_ref[...])
    xall_ref[...] = x
    y = x * lax.rsqrt(jnp.mean(x * x, axis=-1, keepdims=True) + EPS) * g_ref[...]
    sh = jnp.where(is_ctx, sh_ref[1:2, :], sh_ref[0:1, :])
    sc = jnp.where(is_ctx, sc_ref[1:2, :], sc_ref[0:1, :])
    o_ref[...] = (y * (1.0 + sc) + sh).astype(o_ref.dtype)


def _norm_mod_first(ctx2, lat2, gain, mod, layer, which_shift, which_scale, n_ctx_blk):
    d = lat2.shape[1]
    lt = ctx2.shape[0] + lat2.shape[0]
    return pl.pallas_call(
        functools.partial(_norm_mod_first_kernel, n_ctx_blk=n_ctx_blk),
        grid=(lt // BLK,),
        in_specs=[
            pl.BlockSpec((BLK, d), lambda i: (jnp.minimum(i, n_ctx_blk - 1), 0)),
            pl.BlockSpec((BLK, d), lambda i: (jnp.maximum(i - n_ctx_blk, 0), 0)),
            pl.BlockSpec((1, d), lambda i: (0, 0)),
            pl.BlockSpec((None, 8, d), lambda i: (layer, 0, which_shift)),
            pl.BlockSpec((None, 8, d), lambda i: (layer, 0, which_scale)),
        ],
        out_specs=[pl.BlockSpec((BLK, d), lambda i: (i, 0)), pl.BlockSpec((BLK, d), lambda i: (i, 0))],
        out_shape=[jax.ShapeDtypeStruct((lt, d), BF16), jax.ShapeDtypeStruct((lt, d), F32)],
        compiler_params=_cparams(("arbitrary",)),
        name="norm_mod_first",
    )(ctx2, lat2, gain.reshape(1, d), mod, mod)


def _final_norm_kernel(x_ref, g_ref, o_ref):
    x = x_ref[...]
    o_ref[...] = x * lax.rsqrt(jnp.mean(x * x, axis=-1, keepdims=True) + EPS) * g_ref[...]


def _final_norm(x_all, gain, n_ctx_blk):
    lt, d = x_all.shape
    n_lat = lt - n_ctx_blk * BLK
    return pl.pallas_call(
        _final_norm_kernel,
        grid=(n_lat // BLK,),
        in_specs=[
            pl.BlockSpec((BLK, d), lambda i: (i + n_ctx_blk, 0)),
            pl.BlockSpec((1, d), lambda i: (0, 0)),
        ],
        out_specs=pl.BlockSpec((BLK, d), lambda i: (i, 0)),
        out_shape=jax.ShapeDtypeStruct((n_lat, d), F32),
        compiler_params=_cparams(("arbitrary",)),
        name="final_norm",
    )(x_all, gain.reshape(1, d))


def _mm_epilogue(acc, mode, x_ref, gate_ref, o_ref, row0, n_ctx_rows):
    if mode == "plain":
        o_ref[...] = acc.astype(o_ref.dtype)
    elif mode == "relu2":
        r = jnp.maximum(acc, 0.0)
        o_ref[...] = (r * r).astype(o_ref.dtype)
    else:
        row = row0 + lax.broadcasted_iota(jnp.int32, (acc.shape[0], 1), 0)
        gate = jnp.where(row < n_ctx_rows, gate_ref[1:2, :], gate_ref[0:1, :])
        o_ref[...] = x_ref[...] + gate * acc


def _mm_kernel(*refs, mode, nk, tm, n_ctx_rows):
    if mode == "resid":
        a_ref, b_ref, x_ref, gate_ref = refs[:4]
        rest = refs[4:]
    else:
        a_ref, b_ref = refs[:2]
        x_ref = gate_ref = None
        rest = refs[2:]
    o_ref = rest[0]
    row0 = pl.program_id(1) * tm
    if nk == 1:
        acc = _dot(a_ref[...], b_ref[...])
        _mm_epilogue(acc, mode, x_ref, gate_ref, o_ref, row0, n_ctx_rows)
        return
    acc_ref = rest[1]
    k = pl.program_id(2)

    @pl.when(k == 0)
    def _():
        acc_ref[...] = _dot(a_ref[...], b_ref[...])

    if nk > 2:
        @pl.when(jnp.logical_and(k > 0, k < nk - 1))
        def _():
            acc_ref[...] += _dot(a_ref[...], b_ref[...])

    @pl.when(k == nk - 1)
    def _():
        acc = acc_ref[...] + _dot(a_ref[...], b_ref[...])
        _mm_epilogue(acc, mode, x_ref, gate_ref, o_ref, row0, n_ctx_rows)


def _mm_tiles(m, n, k, mode):
    out_bytes = 4 if mode == "resid" else 2
    for tm in (1280, 1024, 768, 512, 256):
        if m % tm:
            continue
        for tn, tk in ((1024, 4096), (768, 4096), (1024, 2048), (512, 4096), (512, 2048), (512, 1024),
                       (256, 1024)):
            if n % tn or k % min(tk, k):
                continue
            tk = min(tk, k)
            use = 2 * (tm * tk * 2 + tk * tn * 2 + tm * tn * out_bytes) + tm * tn * 4
            if mode == "resid":
                use += 2 * tm * tn * 4
            if k > tk:
                use += tm * tn * 4
            if use <= MM_VMEM_BUDGET:
                return tm, tn, tk
    raise ValueError(f"no matmul tiling for {(m, n, k)}")


def _matmul(a, b, layer, mode="plain", x=None, mod=None, which_gate=0, n_ctx_rows=0):
    m, k = a.shape
    n = b.shape[2]
    tm, tn, tk = _mm_tiles(m, n, k, mode)
    nk = k // tk
    in_specs = [
        pl.BlockSpec((tm, tk), lambda j, i, kk: (i, kk)),
        pl.BlockSpec((None, tk, tn), lambda j, i, kk: (layer, kk, j)),
    ]
    args = [a, b]
    if mode == "resid":
        gate_blk0 = which_gate * (n // tn)
        in_specs += [
            pl.BlockSpec((tm, tn), lambda j, i, kk: (i, j)),
            pl.BlockSpec((None, 8, tn), lambda j, i, kk: (layer, 0, gate_blk0 + j)),
        ]
        args += [x, mod]
    scratch = [pltpu.VMEM((tm, tn), F32)] if nk > 1 else []
    return pl.pallas_call(
        functools.partial(_mm_kernel, mode=mode, nk=nk, tm=tm, n_ctx_rows=n_ctx_rows),
        grid=(n // tn, m // tm, nk),
        in_specs=in_specs,
        out_specs=pl.BlockSpec((tm, tn), lambda j, i, kk: (i, j)),
        out_shape=jax.ShapeDtypeStruct((m, n), F32 if mode == "resid" else BF16),
        scratch_shapes=scratch,
        compiler_params=_cparams(("arbitrary", "arbitrary", "arbitrary")),
        name="mm_" + mode,
    )(*args)


def _gates_t_kernel(w_ref, h_ref, o_ref):
    o_ref[...] = _dot_nt(w_ref[...], h_ref[...])


def _gates_t(ws_t, layer, h):
    lt, d = h.shape
    tm = 1280 if lt % 1280 == 0 else BLK
    return pl.pallas_call(
        _gates_t_kernel,
        grid=(lt // tm,),
        in_specs=[
            pl.BlockSpec((None, LANES, d), lambda i: (layer, 0, 0)),
            pl.BlockSpec((tm, d), lambda i: (i, 0)),
        ],
        out_specs=pl.BlockSpec((LANES, tm), lambda i: (0, i)),
        out_shape=jax.ShapeDtypeStruct((LANES, lt), F32),
        compiler_params=_cparams(("arbitrary",)),
        name="gates_t",
    )(ws_t, h)


def _dn_prep_kernel(prev_ref, main_ref, next_ref, w_ref, o_ref, ext_ref, *, n_ctx_blk, n_blk, n_heads):
    i = pl.program_id(0)
    prev_ok = jnp.logical_and(i != 0, i != n_ctx_blk)
    next_ok = jnp.logical_and(i != n_ctx_blk - 1, i != n_blk - 1)
    ext_ref[0:CONV_HALO, :] = jnp.where(prev_ok, prev_ref[...].astype(F32), 0.0)
    ext_ref[CONV_HALO:CONV_HALO + BLK, :] = main_ref[...].astype(F32)
    ext_ref[CONV_HALO + BLK:, :] = jnp.where(next_ok, next_ref[...].astype(F32), 0.0)
    for g in range(3 * n_heads):
        cs = slice(g * HEAD_DIM, (g + 1) * HEAD_DIM)
        acc = None
        for tap in range(CONV_W):
            win = ext_ref[pl.ds(CONV_HALO - CONV_W // 2 + tap, BLK), cs]
            term = win * w_ref[tap:tap + 1, cs]
            acc = term if acc is None else acc + term
        y = _silu(acc)
        if g < 2 * n_heads:
            y = y * lax.rsqrt(jnp.sum(y * y, axis=-1, keepdims=True) + EPS)
            if g < n_heads:
                y = y * HEAD_DIM ** -0.5
        o_ref[:, cs] = y.astype(o_ref.dtype)


def _dn_prep(proj, conv_w, n_heads, n_ctx_blk):
    lt = proj.shape[0]
    c = 3 * n_heads * HEAD_DIM
    n_blk = lt // BLK
    per = BLK // CONV_HALO
    n_halo = lt // CONV_HALO
    return pl.pallas_call(
        functools.partial(_dn_prep_kernel, n_ctx_blk=n_ctx_blk, n_blk=n_blk, n_heads=n_heads),
        grid=(n_blk,),
        in_specs=[
            pl.BlockSpec((CONV_HALO, c), lambda i: (jnp.maximum(i * per - 1, 0), 0)),
            pl.BlockSpec((BLK, c), lambda i: (i, 0)),
            pl.BlockSpec((CONV_HALO, c), lambda i: (jnp.minimum((i + 1) * per, n_halo - 1), 0)),
            pl.BlockSpec((CONV_W, c), lambda i: (0, 0)),
        ],
        out_specs=pl.BlockSpec((BLK, c), lambda i: (i, 0)),
        out_shape=jax.ShapeDtypeStruct((lt, c), BF16),
        scratch_shapes=[pltpu.VMEM((BLK + 2 * CONV_HALO, c), F32)],
        compiler_params=_cparams(("arbitrary",)),
        name="dn_prep",
    )(proj, proj, proj, conv_w)


def _fwd_blk(s):
    return s


def _bwd_blk(s, n_ctx_blk, n_blk):
    return jnp.where(s < n_ctx_blk, n_ctx_blk - 1 - s, n_blk - 1 - (s - n_ctx_blk))


INV_BASE = 8
DN_HEADS_PER_STEP = 4
RET_HEADS_PER_STEP = 5


def _unit_triangular_inverses(a_list):
    ii = lax.broadcasted_iota(jnp.int32, (CHUNK, CHUNK), 0)
    jj = lax.broadcasted_iota(jnp.int32, (CHUNK, CHUNK), 1)

    def mul(x, y):
        return _dot(x.astype(BF16), y.astype(BF16))

    diag_blk = ii // INV_BASE == jj // INV_BASE
    eye_f = jnp.where(ii == jj, 1.0, 0.0)
    d = [jnp.where(diag_blk, a, 0.0) for a in a_list]
    m = [eye_f - x for x in d]
    p = [mul(x, x) for x in d]
    size = 2
    while size < INV_BASE:
        m = [mi + mul(mi, pi) for mi, pi in zip(m, p)]
        size *= 2
        if size < INV_BASE:
            p = [mul(pi, pi) for pi in p]
    size = INV_BASE
    while size < CHUNK:
        pair = jnp.logical_and(ii // (2 * size) == jj // (2 * size), ii // size != jj // size)
        t = [mul(mi, jnp.where(pair, a, 0.0)) for mi, a in zip(m, a_list)]
        m = [mi - mul(ti, mi) for mi, ti in zip(m, t)]
        size *= 2
    return m


def _alternate(per_dir):
    return [it for pair in zip(*per_dir) for it in pair]


def _chunk_order(reverse):
    n_chunks = BLK // CHUNK
    return range(n_chunks - 1, -1, -1) if reverse else range(n_chunks)


def _dn_scan_kernel(alog_ref, dtb_ref, qf, kf, vf, daf, dbf, qb, kb, vb, dab, dbb, of_ref, ob_ref, s_ref,
                    *, hg):
    @pl.when(pl.program_id(1) == 0)
    def _():
        s_ref[...] = jnp.zeros_like(s_ref)

    ii = lax.broadcasted_iota(jnp.int32, (CHUNK, CHUNK), 0)
    jj = lax.broadcasted_iota(jnp.int32, (CHUNK, CHUNK), 1)
    o_refs = (of_ref, ob_ref)
    chains = []
    for d, (q_ref, k_ref, v_ref, da_ref, db_ref) in enumerate(((qf, kf, vf, daf, dbf), (qb, kb, vb, dab, dbb))):
        reverse = d == 1
        incl, strict, eye = _tri_masks(reverse)
        cum = jnp.where((ii >= jj) if reverse else (ii <= jj), 1.0, 0.0).astype(BF16)
        last = 0 if reverse else CHUNK - 1
        for j in range(hg):
            h = pl.program_id(0) * hg + j
            cols = slice(j * HEAD_DIM, (j + 1) * HEAD_DIM)
            neg_a = -jnp.exp(jnp.full((1, CHUNK), alog_ref[d, h], F32))
            g = neg_a * _softplus(da_ref[j] + dtb_ref[d, h])
            beta = _sigmoid(db_ref[j])
            gc_rows = _dot_exact_rhs(g, cum)
            items = []
            for c in _chunk_order(reverse):
                rows = slice(c * CHUNK, (c + 1) * CHUNK)
                gc_r = gc_rows[c:c + 1, :]
                items.append(dict(
                    d=d, chain=d * hg + j, rows=rows, cols=cols, incl=incl, strict=strict, gc_r=gc_r,
                    gc_c=_to_col(gc_r, eye), b_c=_to_col(beta[c:c + 1, :], eye),
                    g_last=gc_r[:, last:last + 1],
                    q=q_ref[rows, cols], k=k_ref[rows, cols], v=v_ref[rows, cols]))
            chains.append(items)
    items = _alternate(chains)

    for it in items:
        it["kk"] = _dot_nt(it["k"], it["k"])
        it["qk"] = _dot_nt(it["q"], it["k"])
    for it in items:
        decay = jnp.exp(jnp.where(it["incl"], it["gc_c"] - it["gc_r"], -jnp.inf))
        it["a"] = jnp.where(it["strict"], it["kk"] * decay, 0.0) * it["b_c"]
        it["qk"] = (it["qk"] * decay).astype(BF16)
        e_gc = jnp.exp(it["gc_c"])
        kf32 = it["k"].astype(F32)
        it["rhs"] = jnp.concatenate(
            [(kf32 * (it["b_c"] * e_gc)).astype(BF16), (it["v"].astype(F32) * it["b_c"]).astype(BF16)], axis=1)
        it["q_dec"] = it["q"].astype(F32) * e_gc
        it["k_dec"] = (kf32 * jnp.exp(it["g_last"] - it["gc_c"])).astype(BF16)
    t_inv = _unit_triangular_inverses([it["a"] for it in items])
    for it, t in zip(items, t_inv):
        it["wu"] = _dot(t.astype(BF16), it["rhs"]).astype(BF16)
    for it in items:
        it["qkwu"] = _dot(it["qk"], it["wu"])
        it["kdwu"] = _dot_tn(it["k_dec"], it["wu"])
    for it in items:
        it["q2"] = (it["q_dec"] - it["qkwu"][:, :HEAD_DIM]).astype(BF16)
        it["m1"] = it["kdwu"][:, :HEAD_DIM].astype(BF16)

    s = [s_ref[n] for n in range(2 * hg)]
    for it in items:
        n = it["chain"]
        s_b = s[n].astype(BF16)
        o = _dot(it["q2"], s_b) + it["qkwu"][:, HEAD_DIM:]
        o_ref = o_refs[it["d"]]
        o_ref[it["rows"], it["cols"]] = o.astype(o_ref.dtype)
        s[n] = jnp.exp(it["g_last"]) * s[n] - _dot(it["m1"], s_b) + it["kdwu"][:, HEAD_DIM:]
    for n in range(2 * hg):
        s_ref[n] = s[n]


def _dn_scan(dnp, gates4, a_log, dt_bias, n_heads, n_ctx_blk):
    lt = dnp.shape[0]
    n_blk = lt // BLK
    bwd = functools.partial(_bwd_blk, n_ctx_blk=n_ctx_blk, n_blk=n_blk)
    n_chunks = BLK // CHUNK
    hg = DN_HEADS_PER_STEP
    assert n_heads % hg == 0
    n_grp = n_heads // hg
    width = hg * HEAD_DIM

    def col(grp_off, blk_fn):
        return pl.BlockSpec((BLK, width), lambda g, s: (blk_fn(s), grp_off + g))

    def gate(grp_off, blk_fn):
        return pl.BlockSpec((hg, None, n_chunks, CHUNK), lambda g, s: (grp_off + g, blk_fn(s), 0, 0))

    smem = pl.BlockSpec(memory_space=pltpu.SMEM)
    out_sds = jax.ShapeDtypeStruct((lt, n_heads * HEAD_DIM), BF16)
    return pl.pallas_call(
        functools.partial(_dn_scan_kernel, hg=hg),
        grid=(n_grp, n_blk),
        in_specs=[
            smem, smem,
            col(0, _fwd_blk), col(n_grp, _fwd_blk), col(2 * n_grp, _fwd_blk),
            gate(0, _fwd_blk), gate(2 * n_grp, _fwd_blk),
            col(0, bwd), col(n_grp, bwd), col(2 * n_grp, bwd),
            gate(n_grp, bwd), gate(3 * n_grp, bwd),
        ],
        out_specs=[
            pl.BlockSpec((BLK, width), lambda g, s: (s, g)),
            pl.BlockSpec((BLK, width), lambda g, s: (bwd(s), g)),
        ],
        out_shape=[out_sds, out_sds],
        scratch_shapes=[pltpu.VMEM((2 * hg, HEAD_DIM, HEAD_DIM), F32)],
        compiler_params=_cparams(("arbitrary", "arbitrary")),
        name="dn_scan",
    )(a_log, dt_bias, dnp, dnp, dnp, gates4, gates4, dnp, dnp, dnp, gates4, gates4)


def _gla_scan_kernel(qf, kf, vf, smf, gwf, gbf, qb, kb, vb, smb, gwb, gbb, of_ref, ob_ref, st_ref, *, dk, pg):
    @pl.when(pl.program_id(1) == 0)
    def _():
        st_ref[...] = jnp.zeros_like(st_ref)

    ii = lax.broadcasted_iota(jnp.int32, (CHUNK, CHUNK), 0)
    jj = lax.broadcasted_iota(jnp.int32, (CHUNK, CHUNK), 1)
    lane = lax.broadcasted_iota(jnp.int32, (1, LANES), 1)
    n_sub = LANES // dk
    sels = [(lane // dk) == sub for sub in range(n_sub)]
    o_refs = (of_ref, ob_ref)
    chains = []
    for d, (q_ref, k_ref, v_ref, sm_ref, gw_ref, gb_ref) in enumerate(
            ((qf, kf, vf, smf, gwf, gbf), (qb, kb, vb, smb, gwb, gbb))):
        reverse = d == 1
        incl, _, _ = _tri_masks(reverse)
        cum = jnp.where((ii <= jj) if reverse else (ii >= jj), 1.0, 0.0).astype(BF16)
        z = _dot(sm_ref[...], gw_ref[...]) + gb_ref[...]
        log_a = (jnp.minimum(z, 0.0) - jnp.log1p(jnp.exp(-jnp.abs(z)))) * (1.0 / GLA_GATE_NORM)
        last = 0 if reverse else CHUNK - 1
        for j in range(pg):
            cols = slice(j * LANES, (j + 1) * LANES)
            vcols = [slice((j * n_sub + sub) * HEAD_DIM, (j * n_sub + sub + 1) * HEAD_DIM) for sub in range(n_sub)]
            items = []
            for c in _chunk_order(reverse):
                rows = slice(c * CHUNK, (c + 1) * CHUNK)
                items.append(dict(d=d, chain=d * pg + j, rows=rows, vcols=vcols, incl=incl, cum=cum, last=last,
                                  log_a=log_a[rows, cols], q=q_ref[rows, cols], k=k_ref[rows, cols],
                                  v=[v_ref[rows, vc] for vc in vcols]))
            chains.append(items)
    items = _alternate(chains)

    for it in items:
        it["gc"] = _dot_exact_lhs(it["cum"], it["log_a"])
    for it in items:
        gc = it["gc"]
        g_last = gc[it["last"]:it["last"] + 1, :]
        kf32 = it["k"].astype(F32)
        q_in = it["q"].astype(F32) * jnp.exp(gc) * dk ** -0.5
        it["qm"] = [jnp.where(sel, q_in, 0.0).astype(BF16) for sel in sels]
        it["k_in"] = (kf32 * jnp.exp(-gc)).astype(BF16)
        it["k_dec"] = (kf32 * jnp.exp(g_last - gc)).astype(BF16)
        it["decay"] = jnp.exp(g_last)
    for it in items:
        it["scores"] = [jnp.where(it["incl"], _dot_nt(qm, it["k_in"]), 0.0).astype(BF16) for qm in it["qm"]]
        upd = None
        for sel, v in zip(sels, it["v"]):
            u = _dot_tn(v, it["k_dec"])
            upd = u if upd is None else jnp.where(sel, u, upd)
        it["upd"] = upd
    for it in items:
        it["intra"] = [_dot(sc, v) for sc, v in zip(it["scores"], it["v"])]

    st = [st_ref[n] for n in range(2 * pg)]
    for it in items:
        n = it["chain"]
        st_b = st[n].astype(BF16)
        o_ref = o_refs[it["d"]]
        for sub in range(n_sub):
            o = it["intra"][sub] + _dot_nt(it["qm"][sub], st_b)
            o_ref[it["rows"], it["vcols"][sub]] = o.astype(o_ref.dtype)
        st[n] = st[n] * it["decay"] + it["upd"]
    for n in range(2 * pg):
        st_ref[n] = st[n]


def _gla_scan(proj, gw_full, gate_b, layer, lay, n_ctx_blk):
    lt = proj.shape[0]
    n_blk = lt // BLK
    n_heads, dk = lay["h_gla"], lay["gla_dk"]
    n_sub = LANES // dk
    pg = n_heads // n_sub
    qw, vw = pg * LANES, pg * n_sub * HEAD_DIM
    assert lay["gq"] % qw == 0 and lay["gk"] % qw == 0 and lay["gv"] % vw == 0
    bwd = functools.partial(_bwd_blk, n_ctx_blk=n_ctx_blk, n_blk=n_blk)
    q0, k0, v0, sm0 = lay["gq"] // qw, lay["gk"] // qw, lay["gv"] // vw, lay["small"] // LANES

    def specs(d, blk_fn):
        return [
            pl.BlockSpec((BLK, qw), lambda p, s: (blk_fn(s), q0 + p)),
            pl.BlockSpec((BLK, qw), lambda p, s: (blk_fn(s), k0 + p)),
            pl.BlockSpec((BLK, vw), lambda p, s: (blk_fn(s), v0 + p)),
            pl.BlockSpec((BLK, LANES), lambda p, s: (blk_fn(s), sm0)),
            pl.BlockSpec((None, None, LANES, qw), lambda p, s: (layer, d, 0, p)),
            pl.BlockSpec((None, None, 1, qw), lambda p, s: (layer, d, 0, p)),
        ]

    out_sds = jax.ShapeDtypeStruct((lt, n_heads * HEAD_DIM), BF16)
    return pl.pallas_call(
        functools.partial(_gla_scan_kernel, dk=dk, pg=pg),
        grid=(n_heads // (n_sub * pg), n_blk),
        in_specs=specs(0, _fwd_blk) + specs(1, bwd),
        out_specs=[
            pl.BlockSpec((BLK, vw), lambda p, s: (s, p)),
            pl.BlockSpec((BLK, vw), lambda p, s: (bwd(s), p)),
        ],
        out_shape=[out_sds, out_sds],
        scratch_shapes=[pltpu.VMEM((2 * pg, HEAD_DIM, LANES), F32)],
        compiler_params=_cparams(("arbitrary", "arbitrary")),
        name="gla_scan",
    )(proj, proj, proj, proj, gw_full, gate_b, proj, proj, proj, proj, gw_full, gate_b)


def _rope(t, cos, sin_signed, first_half):
    partner = jnp.where(first_half, pltpu.roll(t, LANES - HEAD_DIM // 4, 1), pltpu.roll(t, HEAD_DIM // 4, 1))
    return t * cos + partner * sin_signed


def _ret_scan_kernel(lg_ref, qf, kf, vf, cf, sf, qb, kb, vb, cb, sb, of_ref, ob_ref, s_ref, *, hg):
    @pl.when(pl.program_id(1) == 0)
    def _():
        s_ref[...] = jnp.zeros_like(s_ref)

    ii = lax.broadcasted_iota(jnp.int32, (CHUNK, CHUNK), 0)
    jj = lax.broadcasted_iota(jnp.int32, (CHUNK, CHUNK), 1)
    pos = lax.broadcasted_iota(jnp.int32, (CHUNK, 1), 0).astype(F32)
    lane = lax.broadcasted_iota(jnp.int32, (1, LANES), 1)
    first_half = (lane % (HEAD_DIM // 2)) < (HEAD_DIM // 4)
    o_refs = (of_ref, ob_ref)
    lgs = [lg_ref[pl.program_id(0) * hg + j] for j in range(hg)]
    g_chunk = [jnp.exp(jnp.full((1, LANES), lg * CHUNK, F32)) for lg in lgs]
    chains = []
    for d, (q_ref, k_ref, v_ref, cos_ref, sin_ref) in enumerate(((qf, kf, vf, cf, sf), (qb, kb, vb, cb, sb))):
        reverse = d == 1
        incl, _, _ = _tri_masks(reverse)
        dist = ((jj - ii) if reverse else (ii - jj)).astype(F32)
        tables = [(cos_ref[slice(c * CHUNK, (c + 1) * CHUNK), :], sin_ref[slice(c * CHUNK, (c + 1) * CHUNK), :])
                  for c in range(BLK // CHUNK)]
        for j, lg in enumerate(lgs):
            cols = slice(j * HEAD_DIM, (j + 1) * HEAD_DIM)
            decay = jnp.exp(jnp.where(incl, lg * dist, -jnp.inf))
            if reverse:
                q_scale = jnp.exp(lg * (CHUNK - pos))
                k_scale = jnp.exp(lg * pos)
            else:
                q_scale = jnp.exp(lg * (pos + 1.0))
                k_scale = jnp.exp(lg * (CHUNK - 1.0 - pos))
            items = []
            for c in _chunk_order(reverse):
                rows = slice(c * CHUNK, (c + 1) * CHUNK)
                cos, sin = tables[c]
                q = _rope(q_ref[rows, cols].astype(F32) * HEAD_DIM ** -0.5, cos, sin, first_half)
                k = _rope(k_ref[rows, cols].astype(F32), cos, sin, first_half)
                items.append(dict(d=d, chain=d * hg + j, head=j, rows=rows, cols=cols, decay=decay,
                                  q=q.astype(BF16), k=k.astype(BF16),
                                  q_dec=(q * q_scale).astype(BF16), k_dec=(k * k_scale).astype(BF16),
                                  v=v_ref[rows, cols]))
            chains.append(items)
    items = _alternate(chains)

    for it in items:
        it["scores"] = (_dot_nt(it["q"], it["k"]) * it["decay"]).astype(BF16)
        it["upd"] = _dot_tn(it["k_dec"], it["v"])
    for it in items:
        it["intra"] = _dot(it["scores"], it["v"])

    s = [s_ref[n] for n in range(2 * hg)]
    for it in items:
        n = it["chain"]
        o = it["intra"] + _dot(it["q_dec"], s[n].astype(BF16))
        o_ref = o_refs[it["d"]]
        o_ref[it["rows"], it["cols"]] = o.astype(o_ref.dtype)
        s[n] = g_chunk[it["head"]] * s[n] + it["upd"]
    for n in range(2 * hg):
        s_ref[n] = s[n]


def _ret_scan(proj, cos_t, sin_t, log_gamma, lay, n_ctx_blk):
    lt = proj.shape[0]
    n_blk = lt // BLK
    n_heads = lay["h_ret"]
    hg = RET_HEADS_PER_STEP
    width = hg * HEAD_DIM
    assert n_heads % hg == 0 and lay["rq"] % width == 0 and lay["rk"] % width == 0 and lay["rv"] % width == 0
    bwd = functools.partial(_bwd_blk, n_ctx_blk=n_ctx_blk, n_blk=n_blk)
    q0, k0, v0 = lay["rq"] // width, lay["rk"] // width, lay["rv"] // width

    def specs(blk_fn):
        return [
            pl.BlockSpec((BLK, width), lambda g, s: (blk_fn(s), q0 + g)),
            pl.BlockSpec((BLK, width), lambda g, s: (blk_fn(s), k0 + g)),
            pl.BlockSpec((BLK, width), lambda g, s: (blk_fn(s), v0 + g)),
            pl.BlockSpec((BLK, HEAD_DIM), lambda g, s: (blk_fn(s), 0)),
            pl.BlockSpec((BLK, HEAD_DIM), lambda g, s: (blk_fn(s), 0)),
        ]

    out_sds = jax.ShapeDtypeStruct((lt, n_heads * HEAD_DIM), BF16)
    return pl.pallas_call(
        functools.partial(_ret_scan_kernel, hg=hg),
        grid=(n_heads // hg, n_blk),
        in_specs=[pl.BlockSpec(memory_space=pltpu.SMEM)] + specs(_fwd_blk) + specs(bwd),
        out_specs=[
            pl.BlockSpec((BLK, width), lambda g, s: (s, g)),
            pl.BlockSpec((BLK, width), lambda g, s: (bwd(s), g)),
        ],
        out_shape=[out_sds, out_sds],
        scratch_shapes=[pltpu.VMEM((2 * hg, HEAD_DIM, HEAD_DIM), F32)],
        compiler_params=_cparams(("arbitrary", "arbitrary")),
        name="ret_scan",
    )(log_gamma, proj, proj, proj, cos_t, sin_t, proj, proj, proj, cos_t, sin_t)


def _finish_kernel(dnf, dnb, dz, glf, glb, gg, rtf, rtb, rg, g_dn, g_gla, g_ret, o_ref, *, h_dn, h_gla, h_ret):
    col = 0
    for (of, ob, gate, gain, n_heads, centred) in (
        (dnf, dnb, dz, g_dn, h_dn, False),
        (glf, glb, gg, g_gla, h_gla, False),
        (rtf, rtb, rg, g_ret, h_ret, True),
    ):
        for h in range(n_heads):
            cs = slice(h * HEAD_DIM, (h + 1) * HEAD_DIM)
            o = of[:, cs].astype(F32) + ob[:, cs].astype(F32)
            if centred:
                o = o - jnp.mean(o, axis=-1, keepdims=True)
            y = o * lax.rsqrt(jnp.mean(o * o, axis=-1, keepdims=True) + EPS) * gain[...]
            y = y * _silu(gate[:, cs].astype(F32))
            o_ref[:, col:col + HEAD_DIM] = y.astype(o_ref.dtype)
            col += HEAD_DIM


def _finish(proj, dn_o, gla_o, ret_o, g_dn, g_gla, g_ret, lay):
    lt = proj.shape[0]
    h_dn, h_gla, h_ret = lay["h_dn"], lay["h_gla"], lay["h_ret"]
    w_dn, w_gla, w_ret = h_dn * HEAD_DIM, h_gla * HEAD_DIM, h_ret * HEAD_DIM
    d_mix = w_dn + w_gla + w_ret

    def rows(w, cblk=0):
        return pl.BlockSpec((BLK, w), lambda i: (i, cblk))

    gain = pl.BlockSpec((1, HEAD_DIM), lambda i: (0, 0))
    return pl.pallas_call(
        functools.partial(_finish_kernel, h_dn=h_dn, h_gla=h_gla, h_ret=h_ret),
        grid=(lt // BLK,),
        in_specs=[
            rows(w_dn), rows(w_dn), rows(w_dn, lay["dz"] // w_dn),
            rows(w_gla), rows(w_gla), rows(w_gla, lay["gg"] // w_gla),
            rows(w_ret), rows(w_ret), rows(w_ret, lay["rg"] // w_ret),
            gain, gain, gain,
        ],
        out_specs=rows(d_mix),
        out_shape=jax.ShapeDtypeStruct((lt, d_mix), BF16),
        compiler_params=_cparams(("arbitrary",)),
        name="finish",
    )(dn_o[0], dn_o[1], proj, gla_o[0], gla_o[1], proj, ret_o[0], ret_o[1], proj,
      g_dn.reshape(1, HEAD_DIM), g_gla.reshape(1, HEAD_DIM), g_ret.reshape(1, HEAD_DIM))


def _layout(d_model):
    n_heads = d_model // HEAD_DIM
    h_dn = (3 * d_model) // (8 * HEAD_DIM)
    h_gla = (5 * d_model) // (16 * HEAD_DIM)
    h_ret = n_heads - h_dn - h_gla
    gla_dk = HEAD_DIM // 2
    d_dn, d_gk, d_gv, d_ret = h_dn * HEAD_DIM, h_gla * gla_dk, h_gla * HEAD_DIM, h_ret * HEAD_DIM
    sizes = dict(dq=d_dn, dk=d_dn, dv=d_dn, dz=d_dn, da_f=h_dn, da_b=h_dn, db_f=h_dn, db_b=h_dn,
                 gq=d_gk, gk=d_gk, gv=d_gv, gg=d_gv, glr_f=GLA_GATE_RANK, glr_b=GLA_GATE_RANK,
                 rq=d_ret, rk=d_ret, rv=d_ret, rg=d_ret)
    src, off = {}, 0
    for name, size in sizes.items():
        src[name] = (off, size)
        off += size
    assert d_gv == d_ret == 2 * d_gk and 4 * h_dn + 2 * GLA_GATE_RANK <= LANES
    lay = dict(h_dn=h_dn, h_gla=h_gla, h_ret=h_ret, gla_dk=gla_dk, src=src, d_in=off)
    pos = 0
    order = []
    for name in ("dq", "dk", "dv", "dz"):
        lay[name] = pos
        order.append((name, pos))
        pos += d_dn
    lay["small"] = pos
    small_names = ("da_f", "da_b", "db_f", "db_b", "glr_f", "glr_b")
    sp = pos
    for name in small_names:
        lay[name] = sp
        order.append((name, sp))
        sp += sizes[name]
    pos += LANES
    pos = -(-pos // d_gv) * d_gv
    for name in ("gq", "gk"):
        lay[name] = pos
        order.append((name, pos))
        pos += d_gk
    for name in ("gv", "gg", "rq", "rk", "rv", "rg"):
        lay[name] = pos
        order.append((name, pos))
        pos += d_gv
    lay["n_cols"] = pos
    lay["order"] = order
    lay["glr_lane"] = (lay["glr_f"] - lay["small"], lay["glr_b"] - lay["small"])
    return lay


def _relayout_w_in(w_in, lay):
    depth, d, _ = w_in.shape
    pieces, pos = [], 0
    for name, start in sorted(lay["order"], key=lambda e: e[1]):
        if start > pos:
            pieces.append(jnp.zeros((depth, d, start - pos), F32))
        s0, size = lay["src"][name]
        pieces.append(w_in[:, :, s0:s0 + size])
        pos = start + size
    if lay["n_cols"] > pos:
        pieces.append(jnp.zeros((depth, d, lay["n_cols"] - pos), F32))
    return jnp.concatenate(pieces, axis=-1).astype(BF16)


def _rope_tables(n_ctx, n_lat):
    quarter = HEAD_DIM // 4
    inv = (ROPE_BASE ** (-np.arange(quarter, dtype=np.float32) / quarter)).astype(np.float32)
    t = np.arange(n_lat)
    row = ((t // GRID_W).astype(np.float32)[:, None] * inv).astype(np.float64)
    colp = ((t % GRID_W).astype(np.float32)[:, None] * inv).astype(np.float64)
    cos = np.concatenate([np.cos(row), np.cos(row), np.cos(colp), np.cos(colp)], axis=-1)
    sin = np.concatenate([-np.sin(row), np.sin(row), -np.sin(colp), np.sin(colp)], axis=-1)
    cos = np.concatenate([np.ones((n_ctx, HEAD_DIM)), cos], axis=0).astype(np.float32)
    sin = np.concatenate([np.zeros((n_ctx, HEAD_DIM)), sin], axis=0).astype(np.float32)
    return jnp.asarray(cos), jnp.asarray(sin)


def kernel(x, c, ctx, c_ctx, w_ada, b_ada, norm1_g, w_in, dn_conv, dn_a_log, dn_dt_bias, dn_norm_g,
           gla_gate_w, gla_gate_b, gla_norm_g, ret_norm_g, w_out, norm2_g, w_ff1, w_ff2, final_norm_g):
    batch, n_lat, d = x.shape
    n_ctx = ctx.shape[1]
    depth = w_in.shape[0]
    assert batch == 1 and n_ctx % BLK == 0 and n_lat % BLK == 0 and n_lat % GRID_W == 0
    lay = _layout(d)
    assert lay["d_in"] == w_in.shape[2]
    n_ctx_blk = n_ctx // BLK
    lt = n_ctx + n_lat
    h_dn, h_gla, h_ret = lay["h_dn"], lay["h_gla"], lay["h_ret"]

    w_in_b = _relayout_w_in(w_in, lay)
    s0 = lay["src"]["da_f"][0]
    ws_t = jnp.zeros((depth, LANES, d), BF16)
    ws_t = lax.dynamic_update_slice(
        ws_t, jnp.swapaxes(w_in[:, :, s0:s0 + 4 * h_dn], 1, 2).astype(BF16), (0, 0, 0))
    w_out_b = w_out.astype(BF16)
    w_ff1_b = w_ff1.astype(BF16)
    w_ff2_b = w_ff2.astype(BF16)
    gw_full = jnp.zeros((depth, 2, LANES, gla_gate_w.shape[-1]), BF16)
    for dd in range(2):
        gw_full = lax.dynamic_update_slice(
            gw_full, gla_gate_w[:, dd:dd + 1].astype(BF16), (0, dd, lay["glr_lane"][dd], 0))
    gate_b4 = gla_gate_b.reshape(depth, 2, 1, -1)
    cos_t, sin_t = _rope_tables(n_ctx, n_lat)
    log_gamma = jnp.log1p(-jnp.exp2(-5.0 - jnp.arange(h_ret, dtype=F32)))

    c2 = jnp.zeros((8, d), F32).at[0].set(c[0]).at[1].set(c_ctx)
    mod = _ada(c2, w_ada, b_ada)
    x_all = None

    for i in range(depth):
        if i == 0:
            h, x_all = _norm_mod_first(ctx[0], x[0], norm1_g[0], mod, 0, 0, 1, n_ctx_blk)
        else:
            h = _norm_mod(x_all, norm1_g[i], mod, i, 0, 1, n_ctx_blk)
        proj = _matmul(h, w_in_b, i)
        gates4 = _gates_t(ws_t, i, h).reshape(LANES, lt // BLK, BLK // CHUNK, CHUNK)
        dnp = _dn_prep(proj, dn_conv[i], h_dn, n_ctx_blk)
        dn_o = _dn_scan(dnp, gates4, dn_a_log[i], dn_dt_bias[i], h_dn, n_ctx_blk)
        gla_o = _gla_scan(proj, gw_full, gate_b4, i, lay, n_ctx_blk)
        ret_o = _ret_scan(proj, cos_t, sin_t, log_gamma, lay, n_ctx_blk)
        y = _finish(proj, dn_o, gla_o, ret_o, dn_norm_g[i], gla_norm_g[i], ret_norm_g[i], lay)
        x_all = _matmul(y, w_out_b, i, "resid", x_all, mod, 2, n_ctx)
        h2 = _norm_mod(x_all, norm2_g[i], mod, i, 3, 4, n_ctx_blk)
        hid = _matmul(h2, w_ff1_b, i, "relu2")
        x_all = _matmul(hid, w_ff2_b, i, "resid", x_all, mod, 5, n_ctx)
    return _final_norm(x_all, final_norm_g, n_ctx_blk)[None]
```
